```python
import math
import jax, jax.numpy as jnp
from jax import lax
import numpy as np

D_MODEL = 1024
BATCH = 2
SEQ = 8192
DEPTH = 4
DEC_BATCH = 32
DEC_SEQ = 8
PAST_LEN = 8192
PAGE_SIZE = 128

N_MIXERS = 2
N_SB_LAYERS = (DEPTH + 1) // 2
N_CM_LAYERS = DEPTH // 2
N_HEADS = 16
HEAD_DIM = D_MODEL // N_HEADS
Q_BLOCK = 128
CHUNK = 128
CM_WIDTH = D_MODEL
CM_GROUPS = 8
CM_GROUP_DIM = CM_WIDTH // CM_GROUPS
D_FF = int(math.ceil(8 * D_MODEL / 3 / 256)) * 256
PLE_DIM = 256
DEEPNORM_ALPHA = (2 * DEPTH) ** 0.25
DEEPNORM_BETA = (8 * DEPTH) ** -0.25
LN_EPS = 1e-5
SB_BIAS_NEAR = -1.0
SB_BIAS_FAR = -7.0

kernel_name = "stickbreak_chunkmlp_hybrid_step"


def layer_norm(x, g, b):
    xf = x.astype(jnp.float32)
    mu = jnp.mean(xf, axis=-1, keepdims=True)
    var = jnp.mean(jnp.square(xf - mu), axis=-1, keepdims=True)
    y = (xf - mu) * lax.rsqrt(var + LN_EPS) * g.astype(jnp.float32) + b.astype(jnp.float32)
    return y.astype(x.dtype)


def sb_attend(q, k, v, bias, q_pos, k_pos):
    z = jnp.einsum('bqhd,bkhd->bhqk', q, k, preferred_element_type=jnp.float32) * (HEAD_DIM ** -0.5)
    z = z + bias.astype(jnp.float32)[None, :, None, None]
    mask = k_pos[None, :] < q_pos[:, None]
    log_1m = jnp.where(mask, jax.nn.log_sigmoid(-z), 0.0)
    suffix = lax.cumsum(log_1m, axis=3, reverse=True) - log_1m
    a = jnp.where(mask, jnp.exp(jax.nn.log_sigmoid(z) + suffix), 0.0)
    return jnp.einsum('bhqk,bkhd->bqhd', a.astype(v.dtype), v)


def sb_prompt(q, k, v, bias):
    b, s, h, d = q.shape
    nb = s // Q_BLOCK
    qb = q.reshape(b, nb, Q_BLOCK, h, d).swapaxes(0, 1)
    posb = jnp.arange(s, dtype=jnp.int32).reshape(nb, Q_BLOCK)
    k_pos = jnp.arange(s, dtype=jnp.int32)
    out = lax.map(lambda a: sb_attend(a[0], k, v, bias, a[1], k_pos), (qb, posb))
    return out.swapaxes(0, 1).reshape(b, s, h, d)


def split_qkv(x, w_qkv):
    b, s, _ = x.shape
    qkv = (x @ w_qkv).reshape(b, s, 3, N_HEADS, HEAD_DIM)
    return qkv[:, :, 0], qkv[:, :, 1], qkv[:, :, 2]


def chunk_mlp(x, w_uv, ln_v_g, ln_v_b, w_s, b_s):
    z = jax.nn.gelu(x @ w_uv, approximate=False)
    u, v = z[..., :CM_WIDTH], z[..., CM_WIDTH:]
    v = layer_norm(v, ln_v_g, ln_v_b)
    b, s, _ = v.shape
    pad = (-s) % CHUNK
    vp = jnp.pad(v, ((0, 0), (0, pad), (0, 0)))
    nc = (s + pad) // CHUNK
    vc = vp.reshape(b, nc, CHUNK, CM_GROUPS, CM_GROUP_DIM)
    tri = jnp.tril(jnp.ones((CHUNK, CHUNK), dtype=w_s.dtype))
    mixed = jnp.einsum('gts,bnsgc->bntgc', w_s * tri, vc) + b_s.T[:, :, None]
    mixed = mixed.reshape(b, nc * CHUNK, CM_WIDTH)[:, :s]
    return u * mixed, v


def layer_tail(x, mix, p, ln1_g, ln1_b, ln2_g, ln2_b, w_ffn_in, w_ffn_out,
               w_ple_gate, b_ple_gate, w_ple_proj):
    h = layer_norm(DEEPNORM_ALPHA * x + mix, ln1_g, ln1_b)
    gu = h @ w_ffn_in
    f = (jax.nn.silu(gu[..., :D_FF]) * gu[..., D_FF:]) @ w_ffn_out
    h = layer_norm(DEEPNORM_ALPHA * h + f, ln2_g, ln2_b)
    gate = jax.nn.sigmoid(h @ w_ple_gate + b_ple_gate)
    return h + gate * (p @ w_ple_proj)


def setup_inputs(seed: int = 0) -> dict:
    key = jax.random.key(seed)
    ks = jax.random.split(key, 32)
    f32 = jnp.float32
    n_pages = PAST_LEN // PAGE_SIZE
    n_used = DEC_BATCH * n_pages
    n_pool = (5 * n_used + 3) // 4
    nrm = lambda k, shape, scale: jax.random.normal(k, shape, f32) * scale
    perm = jax.random.permutation(ks[4], n_pool)[:n_used]
    page_table = perm.reshape(DEC_BATCH, n_pages).astype(jnp.int32)
    cache_shape = (N_SB_LAYERS, n_pool, PAGE_SIZE, N_HEADS, HEAD_DIM)
    sb_bias = (jnp.linspace(SB_BIAS_NEAR, SB_BIAS_FAR, N_HEADS, dtype=f32)[None, :]
               + nrm(ks[24], (N_SB_LAYERS, N_HEADS), 0.1))
    return {
        "x_prompt": nrm(ks[0], (BATCH, SEQ, D_MODEL), 1.0),
        "x_sample": nrm(ks[1], (DEC_BATCH, DEC_SEQ, D_MODEL), 1.0),
        "cache_k": nrm(ks[2], cache_shape, 1.0),
        "cache_v": nrm(ks[3], cache_shape, 1.0),
        "page_table": page_table,
        "p_prompt": nrm(ks[5], (DEPTH, BATCH, SEQ, PLE_DIM), 1.0),
        "p_sample": nrm(ks[6], (DEPTH, DEC_BATCH, DEC_SEQ, PLE_DIM), 1.0),
        "w_qkv": nrm(ks[7], (N_SB_LAYERS, D_MODEL, 3 * D_MODEL), D_MODEL ** -0.5),
        "w_o_sb": nrm(ks[8], (N_SB_LAYERS, D_MODEL, D_MODEL), D_MODEL ** -0.5 * DEEPNORM_BETA),
        "sb_bias": sb_bias,
        "w_uv": nrm(ks[9], (N_CM_LAYERS, D_MODEL, 2 * CM_WIDTH), D_MODEL ** -0.5),
        "ln_v_g": 1.0 + nrm(ks[10], (N_CM_LAYERS, CM_WIDTH), 0.1),
        "ln_v_b": nrm(ks[11], (N_CM_LAYERS, CM_WIDTH), 0.1),
        "w_s": nrm(ks[12], (N_CM_LAYERS, CM_GROUPS, CHUNK, CHUNK), CHUNK ** -0.5),
        "b_s": 1.0 + nrm(ks[13], (N_CM_LAYERS, CM_GROUPS, CHUNK), 0.1),
        "w_o_cm": nrm(ks[14], (N_CM_LAYERS, CM_WIDTH, D_MODEL), CM_WIDTH ** -0.5 * DEEPNORM_BETA),
        "ln1_g": 1.0 + nrm(ks[15], (DEPTH, D_MODEL), 0.1),
        "ln1_b": nrm(ks[16], (DEPTH, D_MODEL), 0.1),
        "ln2_g": 1.0 + nrm(ks[17], (DEPTH, D_MODEL), 0.1),
        "ln2_b": nrm(ks[18], (DEPTH, D_MODEL), 0.1),
        "w_ffn_in": nrm(ks[19], (DEPTH, D_MODEL, 2 * D_FF), D_MODEL ** -0.5),
        "w_ffn_out": nrm(ks[20], (DEPTH, D_FF, D_MODEL), D_FF ** -0.5 * DEEPNORM_BETA),
        "w_ple_gate": nrm(ks[21], (DEPTH, D_MODEL, D_MODEL), D_MODEL ** -0.5),
        "b_ple_gate": nrm(ks[22], (DEPTH, D_MODEL), 0.1),
        "w_ple_proj": nrm(ks[23], (DEPTH, PLE_DIM, D_MODEL), PLE_DIM ** -0.5),
    }


def reference(x_prompt, x_sample, cache_k, cache_v, page_table, p_prompt, p_sample,
              w_qkv, w_o_sb, sb_bias, w_uv, ln_v_g, ln_v_b, w_s, b_s, w_o_cm,
              ln1_g, ln1_b, ln2_g, ln2_b, w_ffn_in, w_ffn_out,
              w_ple_gate, b_ple_gate, w_ple_proj):
    xp, xs = x_prompt, x_sample
    n_seq, n_pages = page_table.shape
    past_len = n_pages * PAGE_SIZE
    t_new = xs.shape[1]
    q_pos_s = past_len + jnp.arange(t_new, dtype=jnp.int32)
    k_pos_s = jnp.arange(past_len + t_new, dtype=jnp.int32)
    kp_list, vp_list, ks_list, vs_list, cv_list = [], [], [], [], []
    for i in range(DEPTH):
        if i % N_MIXERS == 0:
            j = i // N_MIXERS
            q, k, v = split_qkv(xp, w_qkv[j])
            o = sb_prompt(q, k, v, sb_bias[j])
            mix_p = o.reshape(xp.shape) @ w_o_sb[j]
            kp_list.append(k)
            vp_list.append(v)
            qs, kn, vn = split_qkv(xs, w_qkv[j])
            past_k = cache_k[j][page_table].reshape(n_seq, past_len, N_HEADS, HEAD_DIM)
            past_v = cache_v[j][page_table].reshape(n_seq, past_len, N_HEADS, HEAD_DIM)
            k_all = jnp.concatenate([past_k.astype(kn.dtype), kn], axis=1)
            v_all = jnp.concatenate([past_v.astype(vn.dtype), vn], axis=1)
            os_ = sb_attend(qs, k_all, v_all, sb_bias[j], q_pos_s, k_pos_s)
            mix_s = os_.reshape(xs.shape) @ w_o_sb[j]
            ks_list.append(kn)
            vs_list.append(vn)
        else:
            j = i // N_MIXERS
            g_p, _ = chunk_mlp(xp, w_uv[j], ln_v_g[j], ln_v_b[j], w_s[j], b_s[j])
            mix_p = g_p @ w_o_cm[j]
            g_s, v_s = chunk_mlp(xs, w_uv[j], ln_v_g[j], ln_v_b[j], w_s[j], b_s[j])
            mix_s = g_s @ w_o_cm[j]
            cv_list.append(v_s)
        xp = layer_tail(xp, mix_p, p_prompt[i], ln1_g[i], ln1_b[i], ln2_g[i], ln2_b[i],
                        w_ffn_in[i], w_ffn_out[i], w_ple_gate[i], b_ple_gate[i], w_ple_proj[i])
        xs = layer_tail(xs, mix_s, p_sample[i], ln1_g[i], ln1_b[i], ln2_g[i], ln2_b[i],
                        w_ffn_in[i], w_ffn_out[i], w_ple_gate[i], b_ple_gate[i], w_ple_proj[i])
    k_prompt_new = jnp.stack(kp_list)
    v_prompt_new = jnp.stack(vp_list)
    k_sample_new = jnp.stack(ks_list)
    v_sample_new = jnp.stack(vs_list)
    chunk_v_sample_new = jnp.stack(cv_list)
    return (xp, xs, k_prompt_new, v_prompt_new, k_sample_new, v_sample_new, chunk_v_sample_new)
```

```python
import functools

import jax
import jax.numpy as jnp
from jax import lax
from jax.experimental import pallas as pl
from jax.experimental.pallas import tpu as pltpu

D_MODEL = 1024
N_HEADS = 16
HEAD_DIM = 64
PAGE_SIZE = 128
CHUNK = 128
CM_WIDTH = D_MODEL
CM_GROUPS = 8
CM_GROUP_DIM = CM_WIDTH // CM_GROUPS
D_FF = 2816
PLE_DIM = 256
DEPTH = 4
DEEPNORM_ALPHA = (2 * DEPTH) ** 0.25
LN_EPS = 1e-5

LANES = 128
HEADS_PER_BLOCK = LANES // HEAD_DIM
N_HEAD_BLOCKS = N_HEADS // HEADS_PER_BLOCK
VMEM_LIMIT = 56 * 1024 * 1024

F32 = jnp.float32
BF16 = jnp.bfloat16


def _const_spec(shape):
    nd = len(shape)
    return pl.BlockSpec(shape, lambda *_: (0,) * nd, pipeline_mode=pl.Buffered(1))


def _layer_norm(x, g, b):
    mu = jnp.mean(x, axis=-1, keepdims=True)
    xc = x - mu
    var = jnp.mean(xc * xc, axis=-1, keepdims=True)
    return xc * lax.rsqrt(var + LN_EPS) * g + b


def _neg_softplus(z):
    return jnp.minimum(-z, 0.0) - jnp.log(1.0 + jnp.exp(-jnp.abs(z)))


def _split_bf16(x):
    hi = x.astype(BF16)
    lo = (x - hi.astype(F32)).astype(BF16)
    return hi, lo


def _suffix_sum_rows(tri_t, x):
    hi, lo = _split_bf16(x)
    return (jnp.dot(tri_t, hi, preferred_element_type=F32)
            + jnp.dot(tri_t, lo, preferred_element_type=F32))


def _upper_tri_bf16(n):
    r = lax.broadcasted_iota(jnp.int32, (n, n), 0)
    c = lax.broadcasted_iota(jnp.int32, (n, n), 1)
    return (c >= r).astype(BF16)


def _lower_tri_bf16(n):
    r = lax.broadcasted_iota(jnp.int32, (n, n), 0)
    c = lax.broadcasted_iota(jnp.int32, (n, n), 1)
    return (r >= c).astype(BF16)


def _qkv_kernel(x_ref, w_ref, q_ref, kb_ref, *rest, tk, transposed):
    xb = x_ref[...].astype(BF16)
    qkv = jnp.dot(xb, w_ref[...], preferred_element_type=F32)
    q = qkv[:, :D_MODEL]
    k = qkv[:, D_MODEL:2 * D_MODEL]
    v = qkv[:, 2 * D_MODEL:]
    q_ref[...] = (q * (HEAD_DIM ** -0.5)).astype(BF16)
    kb_ref[...] = k.astype(BF16)
    if transposed:
        kt_ref, vt_ref, vtb_ref = rest
        kt_ref[...] = k.T
        vt = v.T
        vt_ref[...] = vt
        for j in range(vtb_ref.shape[0]):
            vtb_ref[j] = vt[:, j * tk:(j + 1) * tk].astype(BF16)
    else:
        k_ref, v_ref, vb_ref = rest
        k_ref[...] = k
        v_ref[...] = v
        vb_ref[...] = v.astype(BF16)


def _qkv_proj(x, w_bf, *, tm, tk, transposed):
    b, s, d = x.shape
    assert s % tm == 0 and tm % tk == 0
    grid = (b, s // tm)
    row_spec = pl.BlockSpec((None, tm, d), lambda bi, i: (bi, i, 0))
    out_shape = [jax.ShapeDtypeStruct((b, s, d), BF16), jax.ShapeDtypeStruct((b, s, d), BF16)]
    out_specs = [row_spec, row_spec]
    if transposed:
        col_spec = pl.BlockSpec((None, d, tm), lambda bi, i: (bi, 0, i))
        out_shape += [jax.ShapeDtypeStruct((b, d, s), F32), jax.ShapeDtypeStruct((b, d, s), F32),
                      jax.ShapeDtypeStruct((b, s // tk, d, tk), BF16)]
        out_specs += [col_spec, col_spec,
                      pl.BlockSpec((None, tm // tk, d, tk), lambda bi, i: (bi, i, 0, 0))]
    else:
        out_shape += [jax.ShapeDtypeStruct((b, s, d), F32), jax.ShapeDtypeStruct((b, s, d), F32),
                      jax.ShapeDtypeStruct((b, s, d), BF16)]
        out_specs += [row_spec, row_spec, row_spec]
    return pl.pallas_call(
        functools.partial(_qkv_kernel, tk=tk, transposed=transposed),
        grid=grid,
        in_specs=[row_spec, _const_spec(w_bf.shape)],
        out_specs=out_specs,
        out_shape=out_shape,
        compiler_params=pltpu.CompilerParams(
            dimension_semantics=("parallel", "parallel"), vmem_limit_bytes=VMEM_LIMIT),
        name="qkv_proj",
    )(x, w_bf)


def _sb_prompt_kernel(bias_ref, q_ref, k_ref, vt_ref, o_ref, *, tq, tk):
    hb = pl.program_id(1)
    qi = pl.program_id(2)
    q2 = q_ref[...]
    lane = lax.broadcasted_iota(jnp.int32, q2.shape, 1)
    zero_q = jnp.zeros_like(q2)
    q_heads = [jnp.where(lane < HEAD_DIM, q2, zero_q), jnp.where(lane >= HEAD_DIM, q2, zero_q)]
    vrow = lax.broadcasted_iota(jnp.int32, (LANES, tk), 0)
    biases = [bias_ref[hb * HEADS_PER_BLOCK + h] for h in range(HEADS_PER_BLOCK)]
    tri_t = _upper_tri_bf16(tk)
    key_i = lax.broadcasted_iota(jnp.int32, (tk, tq), 0)
    qry_i = lax.broadcasted_iota(jnp.int32, (tk, tq), 1)
    causal = key_i < qry_i

    def tile(kb, carry, masked):
        carries, acc = carry
        ks = pl.multiple_of(kb * tk, tk)
        k2 = k_ref[pl.ds(ks, tk), :]
        vt2 = vt_ref[kb]
        zero_v = jnp.zeros_like(vt2)
        new_carries = []
        for h in range(HEADS_PER_BLOCK):
            s = lax.dot_general(k2, q_heads[h], (((1,), (1,)), ((), ())),
                                preferred_element_type=F32)
            z = s + biases[h]
            log1m = _neg_softplus(z)
            if masked:
                log1m = jnp.where(causal, log1m, 0.0)
            c = _suffix_sum_rows(tri_t, log1m) + carries[h]
            a = jnp.exp(z + c)
            if masked:
                a = jnp.where(causal, a, 0.0)
            in_head = (vrow >= h * HEAD_DIM) & (vrow < (h + 1) * HEAD_DIM)
            v_h = jnp.where(in_head, vt2, zero_v)
            acc = acc + jnp.dot(v_h, a.astype(BF16), preferred_element_type=F32)
            new_carries.append(c[0:1, :])
        return tuple(new_carries), acc

    init = (tuple(jnp.zeros((1, tq), F32) for _ in range(HEADS_PER_BLOCK)),
            jnp.zeros((LANES, tq), F32))
    carry = tile(qi, init, True)
    carry = lax.fori_loop(0, qi, lambda t, cr: tile(qi - 1 - t, cr, False), carry)
    o_ref[...] = carry[1].T.astype(o_ref.dtype)


def _sb_prompt(q_bf, k_bf, vt_bf, bias, *, tq):
    b, s, d = q_bf.shape
    tk = vt_bf.shape[-1]
    assert tq == tk and s % tq == 0
    grid = (b, N_HEAD_BLOCKS, s // tq)
    return pl.pallas_call(
        functools.partial(_sb_prompt_kernel, tq=tq, tk=tk),
        grid=grid,
        in_specs=[
            pl.BlockSpec(memory_space=pltpu.SMEM),
            pl.BlockSpec((None, tq, LANES), lambda bi, hb, i: (bi, i, hb)),
            pl.BlockSpec((None, s, LANES), lambda bi, hb, i: (bi, 0, hb)),
            pl.BlockSpec((None, s // tk, LANES, tk), lambda bi, hb, i: (bi, 0, hb, 0)),
        ],
        out_specs=pl.BlockSpec((None, tq, LANES), lambda bi, hb, i: (bi, i, hb)),
        out_shape=jax.ShapeDtypeStruct((b, s, d), BF16),
        compiler_params=pltpu.CompilerParams(
            dimension_semantics=("parallel", "parallel", "arbitrary"),
            vmem_limit_bytes=VMEM_LIMIT),
        name="sb_prompt",
    )(bias, q_bf, k_bf, vt_bf)


def _suffix_sum_lanes(x, tri):
    hi, lo = _split_bf16(x)
    return (jnp.dot(hi, tri, preferred_element_type=F32)
            + jnp.dot(lo, tri, preferred_element_type=F32))


def _sb_sample_kernel(pt_ref, bias_ref, q_ref, kn_ref, vn_ref, *rest, n_pages_step, t_new):
    k_refs = rest[:n_pages_step]
    v_refs = rest[n_pages_step:2 * n_pages_step]
    o_ref, carry_ref, acc_ref = rest[2 * n_pages_step:]
    g = pl.program_id(1)
    q_rows = q_ref[...]
    bias = bias_ref[...]
    tri = _lower_tri_bf16(PAGE_SIZE)

    def tile(kt_bf, vt_bf, mask):
        s = jnp.dot(q_rows, kt_bf, preferred_element_type=F32)
        z = s + bias
        log1m = _neg_softplus(z)
        if mask is not None:
            log1m = jnp.where(mask, log1m, 0.0)
        c = _suffix_sum_lanes(log1m, tri) + carry_ref[...]
        a = jnp.exp(z + c)
        if mask is not None:
            a = jnp.where(mask, a, 0.0)
        acc_ref[...] += lax.dot_general(a.astype(BF16), vt_bf, (((1,), (1,)), ((), ())),
                                        preferred_element_type=F32)
        carry_ref[...] = jnp.broadcast_to(c[:, 0:1], carry_ref.shape)

    @pl.when(g == 0)
    def _():
        carry_ref[...] = jnp.zeros_like(carry_ref)
        acc_ref[...] = jnp.zeros_like(acc_ref)
        qry_i = lax.broadcasted_iota(jnp.int32, (LANES, PAGE_SIZE), 0) % t_new
        key_i = lax.broadcasted_iota(jnp.int32, (LANES, PAGE_SIZE), 1)
        tile(kn_ref[...], vn_ref[...], key_i < qry_i)

    for i in reversed(range(n_pages_step)):
        tile(k_refs[i][...].astype(BF16), v_refs[i][...].astype(BF16), None)

    @pl.when(g == pl.num_programs(1) - 1)
    def _():
        acc = acc_ref[...]
        r = lax.broadcasted_iota(jnp.int32, acc.shape, 0) // t_new
        c = lax.broadcasted_iota(jnp.int32, acc.shape, 1) // HEAD_DIM
        own = jnp.where(r == c, acc, 0.0).reshape(N_HEADS, t_new, D_MODEL)
        o_ref[...] = jnp.sum(own, axis=0).astype(o_ref.dtype)


def _sb_sample(q_bf, k_bf, v_bf, cache_k, cache_v, layer, page_table, bias, *, pages_per_step):
    n, t_new, d = q_bf.shape
    assert N_HEADS * t_new == LANES
    n_pages = page_table.shape[1]
    assert n_pages % pages_per_step == 0
    n_steps = n_pages // pages_per_step
    n_layers, n_pool = cache_k.shape[:2]
    ck = cache_k.transpose(0, 1, 3, 4, 2).reshape(n_layers, n_pool, d, PAGE_SIZE)
    cv = cache_v.transpose(0, 1, 3, 4, 2).reshape(n_layers, n_pool, d, PAGE_SIZE)
    q4 = q_bf.reshape(n, t_new, N_HEADS, HEAD_DIM)
    eye = jnp.eye(N_HEADS, dtype=BF16)
    q_rows = (q4.transpose(0, 2, 1, 3)[:, :, :, None, :] * eye[None, :, None, :, None]
              ).reshape(n, LANES, d)
    pad = ((0, 0), (0, 0), (0, PAGE_SIZE - t_new))
    knt = jnp.pad(k_bf.transpose(0, 2, 1), pad)
    vnt = jnp.pad(v_bf.transpose(0, 2, 1), pad)
    bias_rows = jnp.broadcast_to(jnp.repeat(bias.astype(F32), t_new)[:, None], (LANES, PAGE_SIZE))

    def page_spec(i):
        return pl.BlockSpec(
            (None, None, d, PAGE_SIZE),
            lambda s, g, pt: (layer, pt[s, (n_steps - 1 - g) * pages_per_step + i], 0, 0))

    seq_spec = lambda shape: pl.BlockSpec((None,) + shape, lambda s, g, pt: (s, 0, 0))
    grid_spec = pltpu.PrefetchScalarGridSpec(
        num_scalar_prefetch=1,
        grid=(n, n_steps),
        in_specs=[pl.BlockSpec((LANES, PAGE_SIZE), lambda s, g, pt: (0, 0)),
                  seq_spec((LANES, d)), seq_spec((d, PAGE_SIZE)), seq_spec((d, PAGE_SIZE))]
                 + [page_spec(i) for i in range(pages_per_step)] * 2,
        out_specs=seq_spec((t_new, d)),
        scratch_shapes=[pltpu.VMEM((LANES, PAGE_SIZE), F32), pltpu.VMEM((LANES, d), F32)],
    )
    return pl.pallas_call(
        functools.partial(_sb_sample_kernel, n_pages_step=pages_per_step, t_new=t_new),
        grid_spec=grid_spec,
        out_shape=jax.ShapeDtypeStruct((n, t_new, d), BF16),
        compiler_params=pltpu.CompilerParams(
            dimension_semantics=("parallel", "arbitrary"), vmem_limit_bytes=VMEM_LIMIT),
        name="sb_sample",
    )(page_table, bias_rows, q_rows, knt, vnt, *([ck] * pages_per_step), *([cv] * pages_per_step))


def _cm_kernel(x_ref, wuv_ref, g_ref, b_ref, ws_ref, bias_ref, gate_ref, *rest, rows, with_v):
    xb = x_ref[...].astype(BF16)
    z = jnp.dot(xb, wuv_ref[...], preferred_element_type=F32)
    z = 0.5 * z * (1.0 + lax.erf(z * (0.5 ** 0.5)))
    u = z[:, :CM_WIDTH]
    v = _layer_norm(z[:, CM_WIDTH:], g_ref[...], b_ref[...])
    if with_v:
        rest[0][...] = v
    r = lax.broadcasted_iota(jnp.int32, (rows, rows), 0)
    c = lax.broadcasted_iota(jnp.int32, (rows, rows), 1)
    w_sp = [jnp.where(c <= r, ws_ref[g], 0.0).astype(BF16) for g in range(CM_GROUPS)]
    vb = v.astype(BF16)
    bias = bias_ref[...]
    for ci in range(x_ref.shape[0] // rows):
        rs = slice(ci * rows, (ci + 1) * rows)
        for g in range(CM_GROUPS):
            cs = slice(g * CM_GROUP_DIM, (g + 1) * CM_GROUP_DIM)
            mixed = jnp.dot(w_sp[g], vb[rs, cs], preferred_element_type=F32) + bias[:, cs]
            gate_ref[rs, cs] = (u[rs, cs] * mixed).astype(gate_ref.dtype)


def _chunk_mlp(x, w_uv_bf, ln_g, ln_b, w_sp, bias_full, *, tm, with_v):
    n, d = x.shape
    rows = w_sp.shape[-1]
    assert n % tm == 0 and tm % rows == 0
    row_spec = pl.BlockSpec((tm, d), lambda i: (i, 0))
    out_shape = [jax.ShapeDtypeStruct((n, CM_WIDTH), BF16)]
    out_specs = [pl.BlockSpec((tm, CM_WIDTH), lambda i: (i, 0))]
    if with_v:
        out_shape.append(jax.ShapeDtypeStruct((n, CM_WIDTH), F32))
        out_specs.append(pl.BlockSpec((tm, CM_WIDTH), lambda i: (i, 0)))
    return pl.pallas_call(
        functools.partial(_cm_kernel, rows=rows, with_v=with_v),
        grid=(n // tm,),
        in_specs=[row_spec, _const_spec(w_uv_bf.shape), _const_spec((1, CM_WIDTH)),
                  _const_spec((1, CM_WIDTH)), _const_spec(w_sp.shape), _const_spec(bias_full.shape)],
        out_specs=out_specs,
        out_shape=out_shape,
        compiler_params=pltpu.CompilerParams(
            dimension_semantics=("parallel",), vmem_limit_bytes=VMEM_LIMIT),
        name="chunk_mlp",
    )(x, w_uv_bf, ln_g.reshape(1, -1), ln_b.reshape(1, -1), w_sp, bias_full)


def _tail_kernel(x_ref, pre_ref, p_ref, wo_ref, g1_ref, b1_ref, win_ref, wout_ref,
                 g2_ref, b2_ref, wg_ref, bg_ref, wp_ref, o_ref):
    mix = jnp.dot(pre_ref[...], wo_ref[...], preferred_element_type=F32)
    h = _layer_norm(DEEPNORM_ALPHA * x_ref[...] + mix, g1_ref[...], b1_ref[...])
    gu = jnp.dot(h.astype(BF16), win_ref[...], preferred_element_type=F32)
    gate_half = gu[:, :D_FF]
    act = gate_half * jax.nn.sigmoid(gate_half) * gu[:, D_FF:]
    f = jnp.dot(act.astype(BF16), wout_ref[...], preferred_element_type=F32)
    h = _layer_norm(DEEPNORM_ALPHA * h + f, g2_ref[...], b2_ref[...])
    ple_gate = jax.nn.sigmoid(
        jnp.dot(h.astype(BF16), wg_ref[...], preferred_element_type=F32) + bg_ref[...])
    ple = jnp.dot(p_ref[...].astype(BF16), wp_ref[...], preferred_element_type=F32)
    o_ref[...] = h + ple_gate * ple


def _layer_tail(x, pre_bf, p, wo, g1, b1, win, wout, g2, b2, wg, bg, wp, *, tm):
    n, d = x.shape
    assert n % tm == 0
    vec = lambda a: a.reshape(1, -1)
    row = lambda w: pl.BlockSpec((tm, w), lambda i: (i, 0))
    consts = [wo, vec(g1), vec(b1), win, wout, vec(g2), vec(b2), wg, vec(bg), wp]
    return pl.pallas_call(
        _tail_kernel,
        grid=(n // tm,),
        in_specs=[row(d), row(d), row(PLE_DIM)] + [_const_spec(c.shape) for c in consts],
        out_specs=row(d),
        out_shape=jax.ShapeDtypeStruct((n, d), F32),
        compiler_params=pltpu.CompilerParams(
            dimension_semantics=("parallel",), vmem_limit_bytes=VMEM_LIMIT),
        name="layer_tail",
    )(x, pre_bf, p, *consts)


def _tile_sizes(n_rows, preferred):
    t = min(preferred, n_rows)
    while n_rows % t:
        t -= 8
    return t


def kernel(x_prompt, x_sample, cache_k, cache_v, page_table, p_prompt, p_sample, w_qkv, w_o_sb,
           sb_bias, w_uv, ln_v_g, ln_v_b, w_s, b_s, w_o_cm, ln1_g, ln1_b, ln2_g, ln2_b,
           w_ffn_in, w_ffn_out, w_ple_gate, b_ple_gate, w_ple_proj):
    b, s, d = x_prompt.shape
    n_seq, t_new, _ = x_sample.shape
    n_p, n_s = b * s, n_seq * t_new
    depth = w_ffn_in.shape[0]
    attn_tile = 256
    tm_p = _tile_sizes(s, 512)
    tm_s = _tile_sizes(n_s, 256)
    pages_per_step = 4 if page_table.shape[1] % 4 == 0 else 1

    xp = x_prompt
    xs = x_sample.reshape(n_s, d)
    kp_list, vp_list, ks_list, vs_list, cv_list = [], [], [], [], []
    for i in range(depth):
        j = i // 2
        if i % 2 == 0:
            wq = w_qkv[j].astype(BF16)
            q_bf, k_bf, kt, vt, vt_bf = _qkv_proj(xp, wq, tm=tm_p, tk=attn_tile, transposed=True)
            pre_p = _sb_prompt(q_bf, k_bf, vt_bf, sb_bias[j], tq=attn_tile).reshape(n_p, d)
            heads_last = lambda a: a.reshape(b, N_HEADS, HEAD_DIM, s).transpose(0, 3, 1, 2)
            kp_list.append(heads_last(kt))
            vp_list.append(heads_last(vt))

            qs_bf, kn_bf, kn, vn, vn_bf = _qkv_proj(
                xs.reshape(1, n_s, d), wq, tm=tm_s, tk=tm_s, transposed=False)
            seq = lambda a: a.reshape(n_seq, t_new, d)
            pre_s = _sb_sample(seq(qs_bf), seq(kn_bf), seq(vn_bf), cache_k, cache_v, j,
                               page_table, sb_bias[j], pages_per_step=pages_per_step
                               ).reshape(n_s, d)
            ks_list.append(kn.reshape(n_seq, t_new, N_HEADS, HEAD_DIM))
            vs_list.append(vn.reshape(n_seq, t_new, N_HEADS, HEAD_DIM))
            w_o = w_o_sb[j].astype(BF16)
        else:
            w_uv_bf = w_uv[j].astype(BF16)
            bias_p = jnp.repeat(b_s[j].T, CM_GROUP_DIM, axis=1)
            pre_p, = _chunk_mlp(xp.reshape(n_p, d), w_uv_bf, ln_v_g[j], ln_v_b[j], w_s[j], bias_p,
                                tm=tm_p, with_v=False)
            w_corner = w_s[j][:, :t_new, :t_new]
            eye = jnp.eye(n_seq, dtype=F32)
            w_bd = (eye[None, :, None, :, None] * w_corner[:, None, :, None, :]
                    ).reshape(CM_GROUPS, n_s, n_s)
            bias_s = jnp.tile(jnp.repeat(b_s[j].T[:t_new], CM_GROUP_DIM, axis=1), (n_seq, 1))
            pre_s, v_s = _chunk_mlp(xs, w_uv_bf, ln_v_g[j], ln_v_b[j], w_bd, bias_s,
                                    tm=n_s, with_v=True)
            cv_list.append(v_s.reshape(n_seq, t_new, CM_WIDTH))
            w_o = w_o_cm[j].astype(BF16)

        tail_w = (w_o, ln1_g[i], ln1_b[i], w_ffn_in[i].astype(BF16), w_ffn_out[i].astype(BF16),
                  ln2_g[i], ln2_b[i], w_ple_gate[i].astype(BF16), b_ple_gate[i],
                  w_ple_proj[i].astype(BF16))
        xp = _layer_tail(xp.reshape(n_p, d), pre_p, p_prompt[i].reshape(n_p, PLE_DIM), *tail_w,
                         tm=_tile_sizes(n_p, 256)).reshape(b, s, d)
        xs = _layer_tail(xs, pre_s, p_sample[i].reshape(n_s, PLE_DIM), *tail_w,
                         tm=_tile_sizes(n_s, 256))
    return (xp, xs.reshape(n_seq, t_new, d), jnp.stack(kp_list), jnp.stack(vp_list),
            jnp.stack(ks_list), jnp.stack(vs_list), jnp.stack(cv_list))
```

```python
import functools

import jax
import jax.numpy as jnp
from jax import lax
from jax.experimental import pallas as pl
from jax.experimental.pallas import tpu as pltpu

D_MODEL = 1024
N_HEADS = 16
HEAD_DIM = 64
PAGE_SIZE = 128
CHUNK = 128
CM_WIDTH = D_MODEL
CM_GROUPS = 8
CM_GROUP_DIM = CM_WIDTH // CM_GROUPS
D_FF = 2816
PLE_DIM = 256
DEPTH = 4
DEEPNORM_ALPHA = (2 * DEPTH) ** 0.25
LN_EPS = 1e-5
LOG2E = 1.4426950408889634

LANES = 128
HEADS_PER_BLOCK = LANES // HEAD_DIM
N_HEAD_BLOCKS = N_HEADS // HEADS_PER_BLOCK
VMEM_LIMIT = 56 * 1024 * 1024
SAMPLE_TILE_PAGES = 4

F32 = jnp.float32
BF16 = jnp.bfloat16


def _const_spec(shape):
    nd = len(shape)
    return pl.BlockSpec(shape, lambda *_: (0,) * nd, pipeline_mode=pl.Buffered(1))


def _layer_norm(x, g, b):
    mu = jnp.mean(x, axis=-1, keepdims=True)
    xc = x - mu
    var = jnp.mean(xc * xc, axis=-1, keepdims=True)
    return xc * lax.rsqrt(var + LN_EPS) * g + b


def _softplus2(z2):
    sign = jnp.uint32(0x80000000)
    neg_abs = lax.bitcast_convert_type(lax.bitcast_convert_type(z2, jnp.uint32) | sign, F32)
    return jnp.maximum(z2, 0.0) + jnp.log2(1.0 + jnp.exp2(neg_abs))


def _split_bf16(x):
    hi = x.astype(BF16)
    lo = (x - hi.astype(F32)).astype(BF16)
    return hi, lo


def _neg_suffix_sum_weights(n):
    r = lax.broadcasted_iota(jnp.int32, (n, n), 0)
    c = lax.broadcasted_iota(jnp.int32, (n, n), 1)
    tri = jnp.where(r >= c, -1.0, 0.0).astype(BF16)
    return jnp.concatenate([tri, tri], axis=0)


def _qkv_kernel(x_ref, w_ref, q_ref, vb_ref, *rest, tk, transposed):
    xb = x_ref[...].astype(BF16)
    qkv = jnp.dot(xb, w_ref[...], preferred_element_type=F32)
    q = qkv[:, :D_MODEL]
    k = qkv[:, D_MODEL:2 * D_MODEL]
    v = qkv[:, 2 * D_MODEL:]
    q_ref[...] = (q * (HEAD_DIM ** -0.5 * LOG2E)).astype(BF16)
    vb_ref[...] = v.astype(BF16)
    if transposed:
        kt_ref, vt_ref, ktb_ref = rest
        kt = k.T
        kt_ref[...] = kt
        vt_ref[...] = v.T
        for j in range(ktb_ref.shape[0]):
            ktb_ref[j] = kt[:, j * tk:(j + 1) * tk].astype(BF16)
    else:
        k_ref, v_ref, kb_ref = rest
        k_ref[...] = k
        v_ref[...] = v
        kb_ref[...] = k.astype(BF16)


def _qkv_proj(x, w_bf, *, tm, tk, transposed):
    b, s, d = x.shape
    assert s % tm == 0 and tm % tk == 0
    grid = (b, s // tm)
    row_spec = pl.BlockSpec((None, tm, d), lambda bi, i: (bi, i, 0))
    out_shape = [jax.ShapeDtypeStruct((b, s, d), BF16), jax.ShapeDtypeStruct((b, s, d), BF16)]
    out_specs = [row_spec, row_spec]
    if transposed:
        col_spec = pl.BlockSpec((None, d, tm), lambda bi, i: (bi, 0, i))
        out_shape += [jax.ShapeDtypeStruct((b, d, s), F32), jax.ShapeDtypeStruct((b, d, s), F32),
                      jax.ShapeDtypeStruct((b, s // tk, d, tk), BF16)]
        out_specs += [col_spec, col_spec,
                      pl.BlockSpec((None, tm // tk, d, tk), lambda bi, i: (bi, i, 0, 0))]
    else:
        out_shape += [jax.ShapeDtypeStruct((b, s, d), F32), jax.ShapeDtypeStruct((b, s, d), F32),
                      jax.ShapeDtypeStruct((b, s, d), BF16)]
        out_specs += [row_spec, row_spec, row_spec]
    return pl.pallas_call(
        functools.partial(_qkv_kernel, tk=tk, transposed=transposed),
        grid=grid,
        in_specs=[row_spec, _const_spec(w_bf.shape)],
        out_specs=out_specs,
        out_shape=out_shape,
        compiler_params=pltpu.CompilerParams(
            dimension_semantics=("parallel", "parallel"), vmem_limit_bytes=VMEM_LIMIT),
        name="qkv_proj",
    )(x, w_bf)


def _sb_prompt_kernel(bias_ref, q_ref, kt_ref, v_ref, o_ref, *, tq, tk):
    hb = pl.program_id(1)
    qi = pl.program_id(2)
    heads = range(HEADS_PER_BLOCK)
    n_band = tq // tk
    tiles_per_trip = 2 - n_band % 2
    q2 = q_ref[...]
    qlane = lax.broadcasted_iota(jnp.int32, q2.shape, 1) // HEAD_DIM
    q_heads = [jnp.where(qlane == h, q2, jnp.zeros_like(q2)) for h in heads]
    vlane = lax.broadcasted_iota(jnp.int32, (tk, LANES), 1) // HEAD_DIM
    biases = [bias_ref[hb * HEADS_PER_BLOCK + h] for h in heads]
    w_cum = _neg_suffix_sum_weights(tk)

    def tile(kb, carry, r0):
        carries, acc = carry
        masked = r0 is not None
        r0 = r0 or 0
        rows = tq - r0
        keep_head_rows = lambda old, new: jnp.concatenate([old[:r0], new], axis=0) if r0 else new
        kt2 = kt_ref[kb]
        v2 = v_ref[pl.ds(pl.multiple_of(kb * tk, tk), tk), :]
        q_st = jnp.concatenate([qh[r0:] for qh in q_heads], axis=0)
        s_all = jnp.dot(q_st, kt2, preferred_element_type=F32)
        if masked:
            causal = (lax.broadcasted_iota(jnp.int32, (rows, tk), 1)
                      < lax.broadcasted_iota(jnp.int32, (rows, tk), 0))
        new_carries, a_parts = [], []
        for h in heads:
            z = s_all[h * rows:(h + 1) * rows] + biases[h]
            sp = _softplus2(z)
            if masked:
                sp = jnp.where(causal, sp, 0.0)
            hi, lo = _split_bf16(sp)
            c = jnp.dot(jnp.concatenate([hi, lo], axis=1), w_cum,
                        preferred_element_type=F32) + carries[h][r0:]
            a = jnp.exp2(z + c)
            if masked:
                a = jnp.where(causal, a, 0.0)
            a_parts.append(a.astype(BF16))
            new_carries.append(keep_head_rows(carries[h], c[:, 0:1]))
        v_st = jnp.concatenate([jnp.where(vlane == h, v2, jnp.zeros_like(v2)) for h in heads], axis=0)
        upd = acc[r0:] + jnp.dot(jnp.concatenate(a_parts, axis=1), v_st, preferred_element_type=F32)
        return tuple(new_carries), keep_head_rows(acc, upd)

    carry = (tuple(jnp.zeros((tq, 1), F32) for _ in heads), jnp.zeros((tq, LANES), F32))
    first_kb = qi * n_band
    for j in reversed(range(n_band)):
        carry = tile(first_kb + j, carry, j * tk)

    def past_tiles(t, cr):
        for j in range(tiles_per_trip):
            cr = tile(first_kb - 1 - (t * tiles_per_trip + j), cr, None)
        return cr

    carry = lax.fori_loop(0, first_kb // tiles_per_trip, past_tiles, carry)
    o_ref[...] = carry[1].astype(o_ref.dtype)


def _sb_prompt(q_bf, kt_bf, v_bf, bias, *, tq):
    b, s, d = q_bf.shape
    tk = kt_bf.shape[-1]
    assert tq % tk == 0 and s % tq == 0
    grid = (b, N_HEAD_BLOCKS, s // tq)
    return pl.pallas_call(
        functools.partial(_sb_prompt_kernel, tq=tq, tk=tk),
        grid=grid,
        in_specs=[
            pl.BlockSpec(memory_space=pltpu.SMEM),
            pl.BlockSpec((None, tq, LANES), lambda bi, hb, i: (bi, i, hb)),
            pl.BlockSpec((None, s // tk, LANES, tk), lambda bi, hb, i: (bi, 0, hb, 0)),
            pl.BlockSpec((None, s, LANES), lambda bi, hb, i: (bi, 0, hb)),
        ],
        out_specs=pl.BlockSpec((None, tq, LANES), lambda bi, hb, i: (bi, i, hb)),
        out_shape=jax.ShapeDtypeStruct((b, s, d), BF16),
        compiler_params=pltpu.CompilerParams(
            dimension_semantics=("parallel", "parallel", "arbitrary"),
            vmem_limit_bytes=VMEM_LIMIT),
        name="sb_prompt",
    )(bias, q_bf, kt_bf, v_bf)


def _sb_sample_kernel(pt_ref, bias_ref, q_ref, kn_ref, vn_ref, *rest, n_pages_step, t_new):
    k_refs = rest[:n_pages_step]
    v_refs = rest[n_pages_step:2 * n_pages_step]
    o_ref, carry_ref, acc_ref = rest[2 * n_pages_step:]
    g = pl.program_id(1)
    q_rows = q_ref[...]
    bias = bias_ref[...]

    def tile(kt_pages, vt_pages, mask):
        n = len(kt_pages)
        s = jnp.concatenate([jnp.dot(q_rows, kt, preferred_element_type=F32) for kt in kt_pages],
                            axis=1)
        z = s + jnp.concatenate([bias] * n, axis=1)
        sp = _softplus2(z)
        if mask is not None:
            sp = jnp.where(mask, sp, 0.0)
        hi, lo = _split_bf16(sp)
        c = jnp.dot(jnp.concatenate([hi, lo], axis=1), _neg_suffix_sum_weights(n * PAGE_SIZE),
                    preferred_element_type=F32)
        c = c + jnp.concatenate([carry_ref[...]] * n, axis=1)
        a = jnp.exp2(z + c)
        if mask is not None:
            a = jnp.where(mask, a, 0.0)
        acc_ref[...] += lax.dot_general(a.astype(BF16), jnp.concatenate(vt_pages, axis=1),
                                        (((1,), (1,)), ((), ())), preferred_element_type=F32)
        carry_ref[...] = jnp.broadcast_to(c[:, 0:1], carry_ref.shape)

    @pl.when(g == 0)
    def _():
        carry_ref[...] = jnp.zeros_like(carry_ref)
        acc_ref[...] = jnp.zeros_like(acc_ref)
        qry_i = lax.broadcasted_iota(jnp.int32, (LANES, PAGE_SIZE), 0) % t_new
        key_i = lax.broadcasted_iota(jnp.int32, (LANES, PAGE_SIZE), 1)
        tile([kn_ref[...]], [vn_ref[...]], key_i < qry_i)

    for lo_page in reversed(range(0, n_pages_step, SAMPLE_TILE_PAGES)):
        sel = slice(lo_page, lo_page + SAMPLE_TILE_PAGES)
        tile([r[...].astype(BF16) for r in k_refs[sel]], [r[...].astype(BF16) for r in v_refs[sel]],
             None)

    @pl.when(g == pl.num_programs(1) - 1)
    def _():
        acc = acc_ref[...]
        r = lax.broadcasted_iota(jnp.int32, acc.shape, 0) // t_new
        c = lax.broadcasted_iota(jnp.int32, acc.shape, 1) // HEAD_DIM
        own = jnp.where(r == c, acc, 0.0).reshape(N_HEADS, t_new, D_MODEL)
        o_ref[...] = jnp.sum(own, axis=0).astype(o_ref.dtype)


def _sb_sample(q_bf, k_bf, v_bf, cache_k, cache_v, layer, page_table, bias, *, pages_per_step):
    n, t_new, d = q_bf.shape
    assert N_HEADS * t_new == LANES
    n_pages = page_table.shape[1]
    assert n_pages % pages_per_step == 0
    n_steps = n_pages // pages_per_step
    n_layers, n_pool = cache_k.shape[:2]
    ck = cache_k.transpose(0, 1, 3, 4, 2).reshape(n_layers, n_pool, d, PAGE_SIZE)
    cv = cache_v.transpose(0, 1, 3, 4, 2).reshape(n_layers, n_pool, d, PAGE_SIZE)
    q4 = q_bf.reshape(n, t_new, N_HEADS, HEAD_DIM)
    eye = jnp.eye(N_HEADS, dtype=BF16)
    q_rows = (q4.transpose(0, 2, 1, 3)[:, :, :, None, :] * eye[None, :, None, :, None]
              ).reshape(n, LANES, d)
    pad = ((0, 0), (0, 0), (0, PAGE_SIZE - t_new))
    knt = jnp.pad(k_bf.transpose(0, 2, 1), pad)
    vnt = jnp.pad(v_bf.transpose(0, 2, 1), pad)
    bias_rows = jnp.broadcast_to(jnp.repeat(bias.astype(F32), t_new)[:, None], (LANES, PAGE_SIZE))

    def page_spec(i):
        return pl.BlockSpec(
            (None, None, d, PAGE_SIZE),
            lambda s, g, pt: (layer, pt[s, (n_steps - 1 - g) * pages_per_step + i], 0, 0))

    seq_spec = lambda shape: pl.BlockSpec((None,) + shape, lambda s, g, pt: (s, 0, 0))
    grid_spec = pltpu.PrefetchScalarGridSpec(
        num_scalar_prefetch=1,
        grid=(n, n_steps),
        in_specs=[pl.BlockSpec((LANES, PAGE_SIZE), lambda s, g, pt: (0, 0)),
                  seq_spec((LANES, d)), seq_spec((d, PAGE_SIZE)), seq_spec((d, PAGE_SIZE))]
                 + [page_spec(i) for i in range(pages_per_step)] * 2,
        out_specs=seq_spec((t_new, d)),
        scratch_shapes=[pltpu.VMEM((LANES, PAGE_SIZE), F32), pltpu.VMEM((LANES, d), F32)],
    )
    return pl.pallas_call(
        functools.partial(_sb_sample_kernel, n_pages_step=pages_per_step, t_new=t_new),
        grid_spec=grid_spec,
        out_shape=jax.ShapeDtypeStruct((n, t_new, d), BF16),
        compiler_params=pltpu.CompilerParams(
            dimension_semantics=("parallel", "arbitrary"), vmem_limit_bytes=VMEM_LIMIT),
        name="sb_sample",
    )(page_table, bias_rows, q_rows, knt, vnt, *([ck] * pages_per_step), *([cv] * pages_per_step))


def _cm_kernel(x_ref, wuv_ref, g_ref, b_ref, ws_ref, bias_ref, gate_ref, *rest, rows, with_v):
    xb = x_ref[...].astype(BF16)
    z = jnp.dot(xb, wuv_ref[...], preferred_element_type=F32)
    z = 0.5 * z * (1.0 + lax.erf(z * (0.5 ** 0.5)))
    u = z[:, :CM_WIDTH]
    v = _layer_norm(z[:, CM_WIDTH:], g_ref[...], b_ref[...])
    if with_v:
        rest[0][...] = v
    r = lax.broadcasted_iota(jnp.int32, (rows, rows), 0)
    c = lax.broadcasted_iota(jnp.int32, (rows, rows), 1)
    w_sp = [jnp.where(c <= r, ws_ref[g], 0.0).astype(BF16) for g in range(CM_GROUPS)]
    vb = v.astype(BF16)
    bias = bias_ref[...]
    for ci in range(x_ref.shape[0] // rows):
        rs = slice(ci * rows, (ci + 1) * rows)
        for g in range(CM_GROUPS):
            cs = slice(g * CM_GROUP_DIM, (g + 1) * CM_GROUP_DIM)
            mixed = jnp.dot(w_sp[g], vb[rs, cs], preferred_element_type=F32) + bias[:, cs]
            gate_ref[rs, cs] = (u[rs, cs] * mixed).astype(gate_ref.dtype)


def _chunk_mlp(x, w_uv_bf, ln_g, ln_b, w_sp, bias_full, *, tm, with_v):
    n, d = x.shape
    rows = w_sp.shape[-1]
    assert n % tm == 0 and tm % rows == 0
    row_spec = pl.BlockSpec((tm, d), lambda i: (i, 0))
    out_shape = [jax.ShapeDtypeStruct((n, CM_WIDTH), BF16)]
    out_specs = [pl.BlockSpec((tm, CM_WIDTH), lambda i: (i, 0))]
    if with_v:
        out_shape.append(jax.ShapeDtypeStruct((n, CM_WIDTH), F32))
        out_specs.append(pl.BlockSpec((tm, CM_WIDTH), lambda i: (i, 0)))
    return pl.pallas_call(
        functools.partial(_cm_kernel, rows=rows, with_v=with_v),
        grid=(n // tm,),
        in_specs=[row_spec, _const_spec(w_uv_bf.shape), _const_spec((1, CM_WIDTH)),
                  _const_spec((1, CM_WIDTH)), _const_spec(w_sp.shape), _const_spec(bias_full.shape)],
        out_specs=out_specs,
        out_shape=out_shape,
        compiler_params=pltpu.CompilerParams(
            dimension_semantics=("parallel",), vmem_limit_bytes=VMEM_LIMIT),
        name="chunk_mlp",
    )(x, w_uv_bf, ln_g.reshape(1, -1), ln_b.reshape(1, -1), w_sp, bias_full)


def _tail_kernel(x_ref, pre_ref, p_ref, wo_ref, g1_ref, b1_ref, win_ref, wout_ref,
                 g2_ref, b2_ref, wg_ref, bg_ref, wp_ref, o_ref):
    mix = jnp.dot(pre_ref[...], wo_ref[...], preferred_element_type=F32)
    h = _layer_norm(DEEPNORM_ALPHA * x_ref[...] + mix, g1_ref[...], b1_ref[...])
    gu = jnp.dot(h.astype(BF16), win_ref[...], preferred_element_type=F32)
    gate_half = gu[:, :D_FF]
    act = gate_half * jax.nn.sigmoid(gate_half) * gu[:, D_FF:]
    f = jnp.dot(act.astype(BF16), wout_ref[...], preferred_element_type=F32)
    h = _layer_norm(DEEPNORM_ALPHA * h + f, g2_ref[...], b2_ref[...])
    ple_gate = jax.nn.sigmoid(
        jnp.dot(h.astype(BF16), wg_ref[...], preferred_element_type=F32) + bg_ref[...])
    ple = jnp.dot(p_ref[...].astype(BF16), wp_ref[...], preferred_element_type=F32)
    o_ref[...] = h + ple_gate * ple


def _layer_tail(x, pre_bf, p, wo, g1, b1, win, wout, g2, b2, wg, bg, wp, *, tm):
    n, d = x.shape
    assert n % tm == 0
    vec = lambda a: a.reshape(1, -1)
    row = lambda w: pl.BlockSpec((tm, w), lambda i: (i, 0))
    consts = [wo, vec(g1), vec(b1), win, wout, vec(g2), vec(b2), wg, vec(bg), wp]
    return pl.pallas_call(
        _tail_kernel,
        grid=(n // tm,),
        in_specs=[row(d), row(d), row(PLE_DIM)] + [_const_spec(c.shape) for c in consts],
        out_specs=row(d),
        out_shape=jax.ShapeDtypeStruct((n, d), F32),
        compiler_params=pltpu.CompilerParams(
            dimension_semantics=("parallel",), vmem_limit_bytes=VMEM_LIMIT),
        name="layer_tail",
    )(x, pre_bf, p, *consts)


def _tile_sizes(n_rows, preferred):
    t = min(preferred, n_rows)
    while n_rows % t:
        t -= 8
    return t


def kernel(x_prompt, x_sample, cache_k, cache_v, page_table, p_prompt, p_sample, w_qkv, w_o_sb,
           sb_bias, w_uv, ln_v_g, ln_v_b, w_s, b_s, w_o_cm, ln1_g, ln1_b, ln2_g, ln2_b,
           w_ffn_in, w_ffn_out, w_ple_gate, b_ple_gate, w_ple_proj):
    b, s, d = x_prompt.shape
    n_seq, t_new, _ = x_sample.shape
    n_p, n_s = b * s, n_seq * t_new
    depth = w_ffn_in.shape[0]
    attn_tile = 256
    attn_tq = _tile_sizes(s, 1024)
    tm_p = _tile_sizes(s, 512)
    tm_s = _tile_sizes(n_s, 256)
    pages_per_step = next(p for p in (8, 4, 2, 1) if page_table.shape[1] % p == 0)

    xp = x_prompt
    xs = x_sample.reshape(n_s, d)
    kp_list, vp_list, ks_list, vs_list, cv_list = [], [], [], [], []
    for i in range(depth):
        j = i // 2
        if i % 2 == 0:
            wq = w_qkv[j].astype(BF16)
            q_bf, v_bf, kt, vt, kt_bf = _qkv_proj(xp, wq, tm=tm_p, tk=attn_tile, transposed=True)
            bias2 = sb_bias[j] * LOG2E
            pre_p = _sb_prompt(q_bf, kt_bf, v_bf, bias2, tq=attn_tq).reshape(n_p, d)
            heads_last = lambda a: a.reshape(b, N_HEADS, HEAD_DIM, s).transpose(0, 3, 1, 2)
            kp_list.append(heads_last(kt))
            vp_list.append(heads_last(vt))

            qs_bf, vn_bf, kn, vn, kn_bf = _qkv_proj(
                xs.reshape(1, n_s, d), wq, tm=tm_s, tk=tm_s, transposed=False)
            seq = lambda a: a.reshape(n_seq, t_new, d)
            pre_s = _sb_sample(seq(qs_bf), seq(kn_bf), seq(vn_bf), cache_k, cache_v, j,
                               page_table, bias2, pages_per_step=pages_per_step
                               ).reshape(n_s, d)
            ks_list.append(kn.reshape(n_seq, t_new, N_HEADS, HEAD_DIM))
            vs_list.append(vn.reshape(n_seq, t_new, N_HEADS, HEAD_DIM))
            w_o = w_o_sb[j].astype(BF16)
        else:
            w_uv_bf = w_uv[j].astype(BF16)
            bias_p = jnp.repeat(b_s[j].T, CM_GROUP_DIM, axis=1)
            pre_p, = _chunk_mlp(xp.reshape(n_p, d), w_uv_bf, ln_v_g[j], ln_v_b[j], w_s[j], bias_p,
                                tm=tm_p, with_v=False)
            w_corner = w_s[j][:, :t_new, :t_new]
            eye = jnp.eye(n_seq, dtype=F32)
            w_bd = (eye[None, :, None, :, None] * w_corner[:, None, :, None, :]
                    ).reshape(CM_GROUPS, n_s, n_s)
            bias_s = jnp.tile(jnp.repeat(b_s[j].T[:t_new], CM_GROUP_DIM, axis=1), (n_seq, 1))
            pre_s, v_s = _chunk_mlp(xs, w_uv_bf, ln_v_g[j], ln_v_b[j], w_bd, bias_s,
                                    tm=n_s, with_v=True)
            cv_list.append(v_s.reshape(n_seq, t_new, CM_WIDTH))
            w_o = w_o_cm[j].astype(BF16)

        tail_w = (w_o, ln1_g[i], ln1_b[i], w_ffn_in[i].astype(BF16), w_ffn_out[i].astype(BF16),
                  ln2_g[i], ln2_b[i], w_ple_gate[i].astype(BF16), b_ple_gate[i],
                  w_ple_proj[i].astype(BF16))
        xp = _layer_tail(xp.reshape(n_p, d), pre_p, p_prompt[i].reshape(n_p, PLE_DIM), *tail_w,
                         tm=_tile_sizes(n_p, 256)).reshape(b, s, d)
        xs = _layer_tail(xs, pre_s, p_sample[i].reshape(n_s, PLE_DIM), *tail_w,
                         tm=_tile_sizes(n_s, 256))
    return (xp, xs.reshape(n_seq, t_new, d), jnp.stack(kp_list), jnp.stack(vp_list),
            jnp.stack(ks_list), jnp.stack(vs_list), jnp.stack(cv_list))
```

```python
import functools

import jax
import jax.numpy as jnp
from jax import lax
from jax.experimental import pallas as pl
from jax.experimental.pallas import tpu as pltpu

D_MODEL = 1024
N_HEADS = 16
HEAD_DIM = 64
PAGE_SIZE = 128
CHUNK = 128
CM_WIDTH = D_MODEL
CM_GROUPS = 8
CM_GROUP_DIM = CM_WIDTH // CM_GROUPS
D_FF = 2816
PLE_DIM = 256
DEPTH = 4
DEEPNORM_ALPHA = (2 * DEPTH) ** 0.25
LN_EPS = 1e-5
LOG2E = 1.4426950408889634

LANES = 128
HEADS_PER_BLOCK = LANES // HEAD_DIM
N_HEAD_BLOCKS = N_HEADS // HEADS_PER_BLOCK
VMEM_LIMIT = 56 * 1024 * 1024
SAMPLE_TILE_PAGES = 4

F32 = jnp.float32
BF16 = jnp.bfloat16


def _const_spec(shape):
    nd = len(shape)
    return pl.BlockSpec(shape, lambda *_: (0,) * nd, pipeline_mode=pl.Buffered(1))


def _layer_norm(x, g, b):
    mu = jnp.mean(x, axis=-1, keepdims=True)
    xc = x - mu
    var = jnp.mean(xc * xc, axis=-1, keepdims=True)
    return xc * lax.rsqrt(var + LN_EPS) * g + b


def _softplus2(z2):
    sign = jnp.uint32(0x80000000)
    neg_abs = lax.bitcast_convert_type(lax.bitcast_convert_type(z2, jnp.uint32) | sign, F32)
    return jnp.maximum(z2, 0.0) + jnp.log2(1.0 + jnp.exp2(neg_abs))


def _split_bf16(x):
    hi = x.astype(BF16)
    lo = (x - hi.astype(F32)).astype(BF16)
    return hi, lo


def _neg_suffix_sum_weights(n):
    r = lax.broadcasted_iota(jnp.int32, (n, n), 0)
    c = lax.broadcasted_iota(jnp.int32, (n, n), 1)
    tri = jnp.where(r >= c, -1.0, 0.0).astype(BF16)
    return jnp.concatenate([tri, tri], axis=0)


Q_SCALE = HEAD_DIM ** -0.5 * LOG2E


def _qkv_prompt_kernel(x_ref, w_ref, q_ref, vb_ref, kt_ref, vt_ref, ktb_ref, *, tk):
    xb = x_ref[...].astype(BF16)
    qkv = jnp.dot(xb, w_ref[...], preferred_element_type=F32)
    k = qkv[:, D_MODEL:2 * D_MODEL]
    v = qkv[:, 2 * D_MODEL:]
    q_ref[...] = (qkv[:, :D_MODEL] * Q_SCALE).astype(BF16)
    vb_ref[...] = v.astype(BF16)
    kt = k.T
    kt_ref[...] = kt
    vt_ref[...] = v.T
    for j in range(ktb_ref.shape[0]):
        ktb_ref[j] = kt[:, j * tk:(j + 1) * tk].astype(BF16)


def _qkv_prompt(x, w_bf, *, tm, tk):
    b, s, d = x.shape
    assert s % tm == 0 and tm % tk == 0
    row_spec = pl.BlockSpec((None, tm, d), lambda bi, i: (bi, i, 0))
    col_spec = pl.BlockSpec((None, d, tm), lambda bi, i: (bi, 0, i))
    return pl.pallas_call(
        functools.partial(_qkv_prompt_kernel, tk=tk),
        grid=(b, s // tm),
        in_specs=[row_spec, _const_spec(w_bf.shape)],
        out_specs=[row_spec, row_spec, col_spec, col_spec,
                   pl.BlockSpec((None, tm // tk, d, tk), lambda bi, i: (bi, i, 0, 0))],
        out_shape=[jax.ShapeDtypeStruct((b, s, d), BF16), jax.ShapeDtypeStruct((b, s, d), BF16),
                   jax.ShapeDtypeStruct((b, d, s), F32), jax.ShapeDtypeStruct((b, d, s), F32),
                   jax.ShapeDtypeStruct((b, s // tk, d, tk), BF16)],
        compiler_params=pltpu.CompilerParams(
            dimension_semantics=("parallel", "parallel"), vmem_limit_bytes=VMEM_LIMIT),
        name="qkv_prompt",
    )(x, w_bf)


def _qkv_sample_kernel(x_ref, w_ref, qrows_ref, k_ref, v_ref, ktb_ref, vb_ref, *, t_new):
    n_s = x_ref.shape[0]
    n_seq = n_s // t_new
    xb = x_ref[...].astype(BF16)
    qkv = jnp.dot(xb, w_ref[...], preferred_element_type=F32)
    k = qkv[:, D_MODEL:2 * D_MODEL]
    v = qkv[:, 2 * D_MODEL:]
    k_ref[...] = k
    v_ref[...] = v
    ktb_ref[...] = k.T.astype(BF16)
    vb_ref[...] = v.astype(BF16)
    q4 = (qkv[:, :D_MODEL] * Q_SCALE).reshape(n_seq, 1, t_new, D_MODEL)
    rows = jnp.broadcast_to(q4, (n_seq, N_HEADS, t_new, D_MODEL)).reshape(n_seq, LANES, D_MODEL)
    r = lax.broadcasted_iota(jnp.int32, (n_seq, LANES, D_MODEL), 1) // t_new
    c = lax.broadcasted_iota(jnp.int32, (n_seq, LANES, D_MODEL), 2) // HEAD_DIM
    qrows_ref[...] = jnp.where(r == c, rows, 0.0).astype(BF16)


def _qkv_sample(x, w_bf, *, t_new):
    n_s, d = x.shape
    assert N_HEADS * t_new == LANES
    whole = lambda shape: pl.BlockSpec(shape, lambda i: (0,) * len(shape))
    out_shape = [jax.ShapeDtypeStruct((n_s // t_new, LANES, d), BF16),
                 jax.ShapeDtypeStruct((n_s, d), F32), jax.ShapeDtypeStruct((n_s, d), F32),
                 jax.ShapeDtypeStruct((d, n_s), BF16), jax.ShapeDtypeStruct((n_s, d), BF16)]
    return pl.pallas_call(
        functools.partial(_qkv_sample_kernel, t_new=t_new),
        grid=(1,),
        in_specs=[whole((n_s, d)), whole(w_bf.shape)],
        out_specs=[whole(o.shape) for o in out_shape],
        out_shape=out_shape,
        compiler_params=pltpu.CompilerParams(
            dimension_semantics=("arbitrary",), vmem_limit_bytes=VMEM_LIMIT),
        name="qkv_sample",
    )(x, w_bf)


def _sb_prompt_kernel(bias_ref, q_ref, kt_ref, v_ref, o_ref, *, tq, tk):
    hb = pl.program_id(1)
    qi = pl.program_id(2)
    heads = range(HEADS_PER_BLOCK)
    n_band = tq // tk
    tiles_per_trip = n_band
    q2 = q_ref[...]
    qlane = lax.broadcasted_iota(jnp.int32, q2.shape, 1) // HEAD_DIM
    q_heads = [jnp.where(qlane == h, q2, jnp.zeros_like(q2)) for h in heads]
    vlane = lax.broadcasted_iota(jnp.int32, (tk, LANES), 1) // HEAD_DIM
    biases = [bias_ref[hb * HEADS_PER_BLOCK + h] for h in heads]
    w_cum = _neg_suffix_sum_weights(tk)[:tk]

    def tile(kb, carry, r0):
        carries, acc = carry
        masked = r0 is not None
        r0 = r0 or 0
        rows = tq - r0
        keep_head_rows = lambda old, new: jnp.concatenate([old[:r0], new], axis=0) if r0 else new
        kt2 = kt_ref[kb]
        v2 = v_ref[pl.ds(pl.multiple_of(kb * tk, tk), tk), :]
        q_st = jnp.concatenate([qh[r0:] for qh in q_heads], axis=0)
        s_all = jnp.dot(q_st, kt2, preferred_element_type=F32)
        if masked:
            causal = (lax.broadcasted_iota(jnp.int32, (rows, tk), 1)
                      < lax.broadcasted_iota(jnp.int32, (rows, tk), 0))
        new_carries, a_parts = [], []
        for h in heads:
            z = s_all[h * rows:(h + 1) * rows] + biases[h]
            sp = _softplus2(z)
            if masked:
                sp = jnp.where(causal, sp, 0.0)
            c = jnp.dot(sp.astype(BF16), w_cum,
                        preferred_element_type=F32) + carries[h][r0:]
            a = jnp.exp2(z + c)
            if masked:
                a = jnp.where(causal, a, 0.0)
            a_parts.append(a.astype(BF16))
            new_carries.append(keep_head_rows(carries[h], c[:, 0:1]))
        v_st = jnp.concatenate([jnp.where(vlane == h, v2, jnp.zeros_like(v2)) for h in heads], axis=0)
        upd = acc[r0:] + jnp.dot(jnp.concatenate(a_parts, axis=1), v_st, preferred_element_type=F32)
        return tuple(new_carries), keep_head_rows(acc, upd)

    carry = (tuple(jnp.zeros((tq, 1), F32) for _ in heads), jnp.zeros((tq, LANES), F32))
    first_kb = qi * n_band
    for j in reversed(range(n_band)):
        carry = tile(first_kb + j, carry, j * tk)

    def past_tiles(t, cr):
        for j in range(tiles_per_trip):
            cr = tile(first_kb - 1 - (t * tiles_per_trip + j), cr, None)
        return cr

    carry = lax.fori_loop(0, first_kb // tiles_per_trip, past_tiles, carry)
    o_ref[...] = carry[1].astype(o_ref.dtype)


def _sb_prompt(q_bf, kt_bf, v_bf, bias, *, tq):
    b, s, d = q_bf.shape
    tk = kt_bf.shape[-1]
    assert tq % tk == 0 and s % tq == 0
    grid = (b, N_HEAD_BLOCKS, s // tq)
    return pl.pallas_call(
        functools.partial(_sb_prompt_kernel, tq=tq, tk=tk),
        grid=grid,
        in_specs=[
            pl.BlockSpec(memory_space=pltpu.SMEM),
            pl.BlockSpec((None, tq, LANES), lambda bi, hb, i: (bi, i, hb)),
            pl.BlockSpec((None, s // tk, LANES, tk), lambda bi, hb, i: (bi, 0, hb, 0)),
            pl.BlockSpec((None, s, LANES), lambda bi, hb, i: (bi, 0, hb)),
        ],
        out_specs=pl.BlockSpec((None, tq, LANES), lambda bi, hb, i: (bi, i, hb)),
        out_shape=jax.ShapeDtypeStruct((b, s, d), BF16),
        compiler_params=pltpu.CompilerParams(
            dimension_semantics=("parallel", "parallel", "arbitrary"),
            vmem_limit_bytes=VMEM_LIMIT),
        name="sb_prompt",
    )(bias, q_bf, kt_bf, v_bf)


def _sb_sample_kernel(pt_ref, bias_ref, q_ref, kn_ref, vn_ref, *rest, n_pages_step, t_new):
    k_refs = rest[:n_pages_step]
    v_refs = rest[n_pages_step:2 * n_pages_step]
    o_ref, carry_ref, acc_ref = rest[2 * n_pages_step:]
    seq = pl.program_id(0)
    g = pl.program_id(1)
    q_rows = q_ref[...]
    bias = bias_ref[...]

    def lane_tiled(x, n):
        return x[:, :n] if n <= LANES else jnp.concatenate([x] * (n // LANES), axis=1)

    def tile(s, mask, weigh_values):
        n = s.shape[1]
        z = s + lane_tiled(bias, n)
        sp = _softplus2(z)
        if mask is not None:
            sp = jnp.where(mask, sp, 0.0)
        hi, lo = _split_bf16(sp)
        c = jnp.dot(jnp.concatenate([hi, lo], axis=1), _neg_suffix_sum_weights(n),
                    preferred_element_type=F32)
        c = c + lane_tiled(carry_ref[...], n)
        a = jnp.exp2(z + c)
        if mask is not None:
            a = jnp.where(mask, a, 0.0)
        acc_ref[...] += weigh_values(a.astype(BF16))
        carry_ref[...] = jnp.broadcast_to(c[:, 0:1], carry_ref.shape)

    @pl.when(g == 0)
    def _():
        carry_ref[...] = jnp.zeros_like(carry_ref)
        acc_ref[...] = jnp.zeros_like(acc_ref)
        kn, vn = kn_ref[...], vn_ref[...]
        n = kn.shape[1]
        qry_i = lax.broadcasted_iota(jnp.int32, (LANES, n), 0) % t_new
        key_i = lax.broadcasted_iota(jnp.int32, (LANES, n), 1)
        mask = (key_i // t_new == seq) & (key_i % t_new < qry_i)
        tile(jnp.dot(q_rows, kn, preferred_element_type=F32), mask,
             lambda a: jnp.dot(a, vn, preferred_element_type=F32))

    for lo_page in reversed(range(0, n_pages_step, SAMPLE_TILE_PAGES)):
        sel = slice(lo_page, lo_page + SAMPLE_TILE_PAGES)
        kt = jnp.concatenate([r[...].astype(BF16) for r in k_refs[sel]], axis=1)
        vt = jnp.concatenate([r[...].astype(BF16) for r in v_refs[sel]], axis=1)
        tile(jnp.dot(q_rows, kt, preferred_element_type=F32), None,
             lambda a: lax.dot_general(a, vt, (((1,), (1,)), ((), ())),
                                       preferred_element_type=F32))

    @pl.when(g == pl.num_programs(1) - 1)
    def _():
        acc = acc_ref[...]
        r = lax.broadcasted_iota(jnp.int32, acc.shape, 0) // t_new
        c = lax.broadcasted_iota(jnp.int32, acc.shape, 1) // HEAD_DIM
        own = jnp.where(r == c, acc, 0.0).reshape(N_HEADS, t_new, D_MODEL)
        o_ref[...] = jnp.sum(own, axis=0).astype(o_ref.dtype)


def _sb_sample(q_rows, kt_new, v_new, cache_k, cache_v, layer, page_table, bias, *, t_new,
               pages_per_step):
    n, _, d = q_rows.shape
    n_pages = page_table.shape[1]
    assert n_pages % pages_per_step == 0
    n_steps = n_pages // pages_per_step
    n_layers, n_pool = cache_k.shape[:2]
    ck = cache_k.transpose(0, 1, 3, 4, 2).reshape(n_layers, n_pool, d, PAGE_SIZE)
    cv = cache_v.transpose(0, 1, 3, 4, 2).reshape(n_layers, n_pool, d, PAGE_SIZE)
    bias_rows = jnp.broadcast_to(jnp.repeat(bias.astype(F32), t_new)[:, None], (LANES, LANES))

    def page_spec(i):
        return pl.BlockSpec(
            (None, None, d, PAGE_SIZE),
            lambda s, g, pt: (layer, pt[s, (n_steps - 1 - g) * pages_per_step + i], 0, 0))

    whole = lambda a: pl.BlockSpec(a.shape, lambda s, g, pt: (0,) * a.ndim)
    grid_spec = pltpu.PrefetchScalarGridSpec(
        num_scalar_prefetch=1,
        grid=(n, n_steps),
        in_specs=[whole(bias_rows), pl.BlockSpec((None, LANES, d), lambda s, g, pt: (s, 0, 0)),
                  whole(kt_new), whole(v_new)]
                 + [page_spec(i) for i in range(pages_per_step)] * 2,
        out_specs=pl.BlockSpec((None, t_new, d), lambda s, g, pt: (s, 0, 0)),
        scratch_shapes=[pltpu.VMEM((LANES, LANES), F32), pltpu.VMEM((LANES, d), F32)],
    )
    return pl.pallas_call(
        functools.partial(_sb_sample_kernel, n_pages_step=pages_per_step, t_new=t_new),
        grid_spec=grid_spec,
        out_shape=jax.ShapeDtypeStruct((n, t_new, d), BF16),
        compiler_params=pltpu.CompilerParams(
            dimension_semantics=("parallel", "arbitrary"), vmem_limit_bytes=VMEM_LIMIT),
        name="sb_sample",
    )(page_table, bias_rows, q_rows, kt_new, v_new,
      *([ck] * pages_per_step), *([cv] * pages_per_step))


def _cm_kernel(x_ref, wuv_ref, g_ref, b_ref, ws_ref, bias_ref, gate_ref, *rest, chunk, with_v):
    rows = ws_ref.shape[-1]
    xb = x_ref[...].astype(BF16)
    z = jnp.dot(xb, wuv_ref[...], preferred_element_type=F32)
    z = 0.5 * z * (1.0 + lax.erf(z * (0.5 ** 0.5)))
    u = z[:, :CM_WIDTH]
    v = _layer_norm(z[:, CM_WIDTH:], g_ref[...], b_ref[...])
    if with_v:
        rest[0][...] = v
    r = lax.broadcasted_iota(jnp.int32, (rows, rows), 0)
    c = lax.broadcasted_iota(jnp.int32, (rows, rows), 1)
    keep = c <= r
    if chunk < rows:
        keep &= (r // chunk) == (c // chunk)
    w_sp = [jnp.where(keep, ws_ref[g], 0.0).astype(BF16) for g in range(CM_GROUPS)]
    vb = v.astype(BF16)
    bias = bias_ref[...]
    for ci in range(x_ref.shape[0] // rows):
        rs = slice(ci * rows, (ci + 1) * rows)
        for g in range(CM_GROUPS):
            cs = slice(g * CM_GROUP_DIM, (g + 1) * CM_GROUP_DIM)
            mixed = jnp.dot(w_sp[g], vb[rs, cs], preferred_element_type=F32) + bias[:, cs]
            gate_ref[rs, cs] = (u[rs, cs] * mixed).astype(gate_ref.dtype)


def _chunk_mlp(x, w_uv_bf, ln_g, ln_b, w_sp, bias_full, *, tm, chunk, with_v):
    n, d = x.shape
    rows = w_sp.shape[-1]
    assert n % tm == 0 and tm % rows == 0 and rows % chunk == 0
    row_spec = pl.BlockSpec((tm, d), lambda i: (i, 0))
    out_shape = [jax.ShapeDtypeStruct((n, CM_WIDTH), BF16)]
    out_specs = [pl.BlockSpec((tm, CM_WIDTH), lambda i: (i, 0))]
    if with_v:
        out_shape.append(jax.ShapeDtypeStruct((n, CM_WIDTH), F32))
        out_specs.append(pl.BlockSpec((tm, CM_WIDTH), lambda i: (i, 0)))
    return pl.pallas_call(
        functools.partial(_cm_kernel, chunk=chunk, with_v=with_v),
        grid=(n // tm,),
        in_specs=[row_spec, _const_spec(w_uv_bf.shape), _const_spec((1, CM_WIDTH)),
                  _const_spec((1, CM_WIDTH)), _const_spec(w_sp.shape), _const_spec(bias_full.shape)],
        out_specs=out_specs,
        out_shape=out_shape,
        compiler_params=pltpu.CompilerParams(
            dimension_semantics=("parallel",), vmem_limit_bytes=VMEM_LIMIT),
        name="chunk_mlp",
    )(x, w_uv_bf, ln_g.reshape(1, -1), ln_b.reshape(1, -1), w_sp, bias_full)


def _tail_kernel(x_ref, pre_ref, p_ref, wo_ref, g1_ref, b1_ref, win_ref, wout_ref,
                 g2_ref, b2_ref, wg_ref, bg_ref, wp_ref, o_ref):
    mix = jnp.dot(pre_ref[...], wo_ref[...], preferred_element_type=F32)
    h = _layer_norm(DEEPNORM_ALPHA * x_ref[...] + mix, g1_ref[...], b1_ref[...])
    gu = jnp.dot(h.astype(BF16), win_ref[...], preferred_element_type=F32)
    gate_half = gu[:, :D_FF]
    act = gate_half * jax.nn.sigmoid(gate_half) * gu[:, D_FF:]
    f = jnp.dot(act.astype(BF16), wout_ref[...], preferred_element_type=F32)
    h = _layer_norm(DEEPNORM_ALPHA * h + f, g2_ref[...], b2_ref[...])
    ple_gate = jax.nn.sigmoid(
        jnp.dot(h.astype(BF16), wg_ref[...], preferred_element_type=F32) + bg_ref[...])
    ple = jnp.dot(p_ref[...].astype(BF16), wp_ref[...], preferred_element_type=F32)
    o_ref[...] = h + ple_gate * ple


def _layer_tail(x, pre_bf, p, wo, g1, b1, win, wout, g2, b2, wg, bg, wp, *, tm):
    n, d = x.shape
    assert n % tm == 0
    vec = lambda a: a.reshape(1, -1)
    row = lambda w: pl.BlockSpec((tm, w), lambda i: (i, 0))
    consts = [wo, vec(g1), vec(b1), win, wout, vec(g2), vec(b2), wg, vec(bg), wp]
    return pl.pallas_call(
        _tail_kernel,
        grid=(n // tm,),
        in_specs=[row(d), row(d), row(PLE_DIM)] + [_const_spec(c.shape) for c in consts],
        out_specs=row(d),
        out_shape=jax.ShapeDtypeStruct((n, d), F32),
        compiler_params=pltpu.CompilerParams(
            dimension_semantics=("parallel",), vmem_limit_bytes=VMEM_LIMIT),
        name="layer_tail",
    )(x, pre_bf, p, *consts)


def _tile_sizes(n_rows, preferred):
    t = min(preferred, n_rows)
    while n_rows % t:
        t -= 8
    return t


def kernel(x_prompt, x_sample, cache_k, cache_v, page_table, p_prompt, p_sample, w_qkv, w_o_sb,
           sb_bias, w_uv, ln_v_g, ln_v_b, w_s, b_s, w_o_cm, ln1_g, ln1_b, ln2_g, ln2_b,
           w_ffn_in, w_ffn_out, w_ple_gate, b_ple_gate, w_ple_proj):
    b, s, d = x_prompt.shape
    n_seq, t_new, _ = x_sample.shape
    n_p, n_s = b * s, n_seq * t_new
    depth = w_ffn_in.shape[0]
    attn_tile = 256
    attn_tq = _tile_sizes(s, 1024)
    tm_p = _tile_sizes(s, 512)
    pages_per_step = next(p for p in (8, 4, 2, 1) if page_table.shape[1] % p == 0)

    xp = x_prompt
    xs = x_sample.reshape(n_s, d)
    kp_list, vp_list, ks_list, vs_list, cv_list = [], [], [], [], []
    for i in range(depth):
        j = i // 2
        if i % 2 == 0:
            wq = w_qkv[j].astype(BF16)
            q_bf, v_bf, kt, vt, kt_bf = _qkv_prompt(xp, wq, tm=tm_p, tk=attn_tile)
            bias2 = sb_bias[j] * LOG2E
            pre_p = _sb_prompt(q_bf, kt_bf, v_bf, bias2, tq=attn_tq).reshape(n_p, d)
            heads_last = lambda a: a.reshape(b, N_HEADS, HEAD_DIM, s).transpose(0, 3, 1, 2)
            kp_list.append(heads_last(kt))
            vp_list.append(heads_last(vt))

            q_rows, kn, vn, knt_bf, vn_bf = _qkv_sample(xs, wq, t_new=t_new)
            pre_s = _sb_sample(q_rows, knt_bf, vn_bf, cache_k, cache_v, j, page_table, bias2,
                               t_new=t_new, pages_per_step=pages_per_step).reshape(n_s, d)
            ks_list.append(kn.reshape(n_seq, t_new, N_HEADS, HEAD_DIM))
            vs_list.append(vn.reshape(n_seq, t_new, N_HEADS, HEAD_DIM))
            w_o = w_o_sb[j].astype(BF16)
        else:
            w_uv_bf = w_uv[j].astype(BF16)
            bias_p = jnp.repeat(b_s[j].T, CM_GROUP_DIM, axis=1)
            pre_p, = _chunk_mlp(xp.reshape(n_p, d), w_uv_bf, ln_v_g[j], ln_v_b[j], w_s[j], bias_p,
                                tm=tm_p, chunk=CHUNK, with_v=False)
            w_rep = jnp.tile(w_s[j][:, :t_new, :t_new], (1, n_seq, n_seq))
            bias_s = jnp.tile(jnp.repeat(b_s[j].T[:t_new], CM_GROUP_DIM, axis=1), (n_seq, 1))
            pre_s, v_s = _chunk_mlp(xs, w_uv_bf, ln_v_g[j], ln_v_b[j], w_rep, bias_s,
                                    tm=n_s, chunk=t_new, with_v=True)
            cv_list.append(v_s.reshape(n_seq, t_new, CM_WIDTH))
            w_o = w_o_cm[j].astype(BF16)

        tail_w = (w_o, ln1_g[i], ln1_b[i], w_ffn_in[i].astype(BF16), w_ffn_out[i].astype(BF16),
                  ln2_g[i], ln2_b[i], w_ple_gate[i].astype(BF16), b_ple_gate[i],
                  w_ple_proj[i].astype(BF16))
        xp = _layer_tail(xp.reshape(n_p, d), pre_p, p_prompt[i].reshape(n_p, PLE_DIM), *tail_w,
                         tm=_tile_sizes(n_p, 256)).reshape(b, s, d)
        xs = _layer_tail(xs, pre_s, p_sample[i].reshape(n_s, PLE_DIM), *tail_w,
                         tm=_tile_sizes(n_s, 256))
    return (xp, xs.reshape(n_seq, t_new, d), jnp.stack(kp_list), jnp.stack(vp_list),
            jnp.stack(ks_list), jnp.stack(vs_list), jnp.stack(cv_list))
```

```python
import functools

import jax
import jax.numpy as jnp
from jax import lax
from jax.experimental import pallas as pl
from jax.experimental.pallas import tpu as pltpu

D_MODEL = 1024
N_HEADS = 16
HEAD_DIM = 64
PAGE_SIZE = 128
CHUNK = 128
CM_WIDTH = D_MODEL
CM_GROUPS = 8
CM_GROUP_DIM = CM_WIDTH // CM_GROUPS
D_FF = 2816
PLE_DIM = 256
DEPTH = 4
DEEPNORM_ALPHA = (2 * DEPTH) ** 0.25
LN_EPS = 1e-5
LOG2E = 1.4426950408889634

LANES = 128
HEADS_PER_BLOCK = LANES // HEAD_DIM
N_HEAD_BLOCKS = N_HEADS // HEADS_PER_BLOCK
VMEM_LIMIT = 56 * 1024 * 1024
SAMPLE_TILE_PAGES = 4
EXIT_LOG2 = 160.0

F32 = jnp.float32
BF16 = jnp.bfloat16


def _const_spec(shape):
    nd = len(shape)
    return pl.BlockSpec(shape, lambda *_: (0,) * nd, pipeline_mode=pl.Buffered(1))


def _layer_spec(stacked, layer):
    nd = stacked.ndim
    return pl.BlockSpec((None,) + stacked.shape[1:], lambda *_: (layer,) + (0,) * (nd - 1),
                        pipeline_mode=pl.Buffered(1))


def _layer_norm(x, g, b):
    mu = jnp.mean(x, axis=-1, keepdims=True)
    xc = x - mu
    var = jnp.mean(xc * xc, axis=-1, keepdims=True)
    return xc * lax.rsqrt(var + LN_EPS) * g + b


def _softplus2(z2):
    sign = jnp.uint32(0x80000000)
    neg_abs = lax.bitcast_convert_type(lax.bitcast_convert_type(z2, jnp.uint32) | sign, F32)
    return jnp.maximum(z2, 0.0) + jnp.log2(1.0 + jnp.exp2(neg_abs))


def _split_bf16(x):
    hi = x.astype(BF16)
    lo = (x - hi.astype(F32)).astype(BF16)
    return hi, lo


def _neg_suffix_sum_weights(n):
    r = lax.broadcasted_iota(jnp.int32, (n, n), 0)
    c = lax.broadcasted_iota(jnp.int32, (n, n), 1)
    tri = jnp.where(r >= c, -1.0, 0.0).astype(BF16)
    return jnp.concatenate([tri, tri], axis=0)


Q_SCALE = HEAD_DIM ** -0.5 * LOG2E


def _qkv_prompt_kernel(x_ref, w_ref, q_ref, vb_ref, kt_ref, vt_ref, ktb_ref, *, tk):
    xb = x_ref[...].astype(BF16)
    qkv = jnp.dot(xb, w_ref[...], preferred_element_type=F32)
    k = qkv[:, D_MODEL:2 * D_MODEL]
    v = qkv[:, 2 * D_MODEL:]
    q_ref[...] = (qkv[:, :D_MODEL] * Q_SCALE).astype(BF16)
    vb_ref[...] = v.astype(BF16)
    kt = k.T
    kt_ref[...] = kt
    vt_ref[...] = v.T
    for j in range(ktb_ref.shape[0]):
        ktb_ref[j] = kt[:, j * tk:(j + 1) * tk].astype(BF16)


def _qkv_prompt(x, w_bf, layer, *, tm, tk):
    b, s, d = x.shape
    assert s % tm == 0 and tm % tk == 0
    row_spec = pl.BlockSpec((None, tm, d), lambda bi, i: (bi, i, 0))
    col_spec = pl.BlockSpec((None, d, tm), lambda bi, i: (bi, 0, i))
    return pl.pallas_call(
        functools.partial(_qkv_prompt_kernel, tk=tk),
        grid=(b, s // tm),
        in_specs=[row_spec, _layer_spec(w_bf, layer)],
        out_specs=[row_spec, row_spec, col_spec, col_spec,
                   pl.BlockSpec((None, tm // tk, d, tk), lambda bi, i: (bi, i, 0, 0))],
        out_shape=[jax.ShapeDtypeStruct((b, s, d), BF16), jax.ShapeDtypeStruct((b, s, d), BF16),
                   jax.ShapeDtypeStruct((b, d, s), F32), jax.ShapeDtypeStruct((b, d, s), F32),
                   jax.ShapeDtypeStruct((b, s // tk, d, tk), BF16)],
        compiler_params=pltpu.CompilerParams(
            dimension_semantics=("parallel", "parallel"), vmem_limit_bytes=VMEM_LIMIT),
        name="qkv_prompt",
    )(x, w_bf)


def _qkv_sample_kernel(x_ref, w_ref, qrows_ref, k_ref, v_ref, ktb_ref, vb_ref, *, t_new):
    n_s = x_ref.shape[0]
    n_seq = n_s // t_new
    xb = x_ref[...].astype(BF16)
    qkv = jnp.dot(xb, w_ref[...], preferred_element_type=F32)
    k = qkv[:, D_MODEL:2 * D_MODEL]
    v = qkv[:, 2 * D_MODEL:]
    k_ref[...] = k
    v_ref[...] = v
    ktb_ref[...] = k.T.astype(BF16)
    vb_ref[...] = v.astype(BF16)
    q4 = (qkv[:, :D_MODEL] * Q_SCALE).reshape(n_seq, 1, t_new, D_MODEL)
    rows = jnp.broadcast_to(q4, (n_seq, N_HEADS, t_new, D_MODEL)).reshape(n_seq, LANES, D_MODEL)
    r = lax.broadcasted_iota(jnp.int32, (n_seq, LANES, D_MODEL), 1) // t_new
    c = lax.broadcasted_iota(jnp.int32, (n_seq, LANES, D_MODEL), 2) // HEAD_DIM
    qrows_ref[...] = jnp.where(r == c, rows, 0.0).astype(BF16)


def _qkv_sample(x, w_bf, layer, *, t_new):
    n_s, d = x.shape
    assert N_HEADS * t_new == LANES
    whole = lambda shape: pl.BlockSpec(shape, lambda i: (0,) * len(shape))
    out_shape = [jax.ShapeDtypeStruct((n_s // t_new, LANES, d), BF16),
                 jax.ShapeDtypeStruct((n_s, d), F32), jax.ShapeDtypeStruct((n_s, d), F32),
                 jax.ShapeDtypeStruct((d, n_s), BF16), jax.ShapeDtypeStruct((n_s, d), BF16)]
    return pl.pallas_call(
        functools.partial(_qkv_sample_kernel, t_new=t_new),
        grid=(1,),
        in_specs=[whole((n_s, d)), _layer_spec(w_bf, layer)],
        out_specs=[whole(o.shape) for o in out_shape],
        out_shape=out_shape,
        compiler_params=pltpu.CompilerParams(
            dimension_semantics=("arbitrary",), vmem_limit_bytes=VMEM_LIMIT),
        name="qkv_sample",
    )(x, w_bf)


def _sb_prompt_kernel(bias_ref, q_ref, kt_ref, v_ref, o_ref, knorm_ref, *, tq, tk):
    hb = pl.program_id(1)
    qi = pl.program_id(2)
    heads = range(HEADS_PER_BLOCK)
    n_band = tq // tk
    tiles_per_trip = n_band
    q2 = q_ref[...]
    qlane_i = lax.broadcasted_iota(jnp.int32, q2.shape, 1)
    qlane = qlane_i // HEAD_DIM
    vlane = lax.broadcasted_iota(jnp.int32, (tk, LANES), 1) // HEAD_DIM
    krow_i = lax.broadcasted_iota(jnp.int32, (LANES, tk), 0)
    biases = [bias_ref[hb * HEADS_PER_BLOCK + h] for h in heads]
    w_cum = _neg_suffix_sum_weights(tk)[:tk]
    q_heads, k_own_rows, k_bias_rows = [], [], []
    for h in heads:
        spare = (HEADS_PER_BLOCK - 1 - h) * HEAD_DIM
        ones = ((qlane_i == spare) | (qlane_i == spare + 1)).astype(BF16)
        q_heads.append(jnp.where(qlane == h, q2, ones))
        b_full = jnp.zeros((LANES, tk), F32) + biases[h]
        b_hi = b_full.astype(BF16)
        b_lo = (b_full - b_hi.astype(F32)).astype(BF16)
        k_own_rows.append(jnp.where(krow_i // HEAD_DIM == h, 1.0, 0.0).astype(BF16))
        k_bias_rows.append(jnp.where(krow_i == spare, b_full,
                                     jnp.where(krow_i == spare + 1, b_full - b_hi.astype(F32), 0.0)
                                     ).astype(BF16))

    def tile(kb, carry, r0):
        carries, acc = carry
        masked = r0 is not None
        r0 = r0 or 0
        rows = tq - r0
        keep_head_rows = lambda old, new: jnp.concatenate([old[:r0], new], axis=0) if r0 else new
        kt2 = kt_ref[kb]
        v2 = v_ref[pl.ds(pl.multiple_of(kb * tk, tk), tk), :]
        if masked:
            causal = (lax.broadcasted_iota(jnp.int32, (rows, tk), 1)
                      < lax.broadcasted_iota(jnp.int32, (rows, tk), 0))
        new_carries, a_parts = [], []
        for h in heads:
            kt_h = kt2 * k_own_rows[h] + k_bias_rows[h]
            z = jnp.dot(q_heads[h][r0:], kt_h, preferred_element_type=F32)
            sp = _softplus2(z)
            if masked:
                sp = jnp.where(causal, sp, 0.0)
            c = jnp.dot(sp.astype(BF16), w_cum,
                        preferred_element_type=F32) + carries[h][r0:]
            a = jnp.exp2(z + c)
            if masked:
                a = jnp.where(causal, a, 0.0)
            a_parts.append(a.astype(BF16))
            new_carries.append(keep_head_rows(carries[h], c[:, 0:1]))
        v_st = jnp.concatenate([jnp.where(vlane == h, v2, jnp.zeros_like(v2)) for h in heads], axis=0)
        upd = acc[r0:] + jnp.dot(jnp.concatenate(a_parts, axis=1), v_st, preferred_element_type=F32)
        return tuple(new_carries), keep_head_rows(acc, upd)

    carry = (tuple(jnp.zeros((tq, 1), F32) for _ in heads), jnp.zeros((tq, LANES), F32))
    first_kb = qi * n_band
    for j in reversed(range(n_band)):
        carry = tile(first_kb + j, carry, j * tk)

    @pl.when(qi == 0)
    def _():
        kt_all = kt_ref[...].astype(F32)
        k_sq = kt_all * kt_all
        for h in heads:
            per_key = jnp.sum(k_sq[:, h * HEAD_DIM:(h + 1) * HEAD_DIM, :], axis=1)
            knorm_ref[h] = jnp.max(per_key)

    q_sq = q2.astype(F32) * q2.astype(F32)
    z_max = []
    for h in heads:
        q_norm2 = jnp.sum(jnp.where(qlane == h, q_sq, 0.0), axis=1, keepdims=True)
        z_max.append(jnp.sqrt(q_norm2 * knorm_ref[h]) * 1.01 + jnp.abs(jnp.zeros((tq, 1), F32)
                                                                     + biases[h]))

    def log2_weight_bound(carries):
        return jnp.max(jnp.maximum(*(carries[h] + z_max[h] for h in heads)))

    def more_to_do(state):
        t, bound, _ = state
        return (t < first_kb // tiles_per_trip) & (bound > -EXIT_LOG2)

    def past_tiles(state):
        t, _, cr = state
        for j in range(tiles_per_trip):
            cr = tile(first_kb - 1 - (t * tiles_per_trip + j), cr, None)
        return t + 1, log2_weight_bound(cr[0]), cr

    _, _, carry = lax.while_loop(more_to_do, past_tiles,
                                 (jnp.int32(0), log2_weight_bound(carry[0]), carry))
    o_ref[...] = carry[1].astype(o_ref.dtype)


def _sb_prompt(q_bf, kt_bf, v_bf, bias, *, tq):
    b, s, d = q_bf.shape
    tk = kt_bf.shape[-1]
    assert tq % tk == 0 and s % tq == 0
    grid = (b, N_HEAD_BLOCKS, s // tq)
    return pl.pallas_call(
        functools.partial(_sb_prompt_kernel, tq=tq, tk=tk),
        grid=grid,
        in_specs=[
            pl.BlockSpec(memory_space=pltpu.SMEM),
            pl.BlockSpec((None, tq, LANES), lambda bi, hb, i: (bi, i, hb)),
            pl.BlockSpec((None, s // tk, LANES, tk), lambda bi, hb, i: (bi, 0, hb, 0)),
            pl.BlockSpec((None, s, LANES), lambda bi, hb, i: (bi, 0, hb)),
        ],
        out_specs=pl.BlockSpec((None, tq, LANES), lambda bi, hb, i: (bi, i, hb)),
        out_shape=jax.ShapeDtypeStruct((b, s, d), BF16),
        scratch_shapes=[pltpu.SMEM((HEADS_PER_BLOCK,), F32)],
        compiler_params=pltpu.CompilerParams(
            dimension_semantics=("parallel", "parallel", "arbitrary"),
            vmem_limit_bytes=VMEM_LIMIT),
        name="sb_prompt",
    )(bias, q_bf, kt_bf, v_bf)


def _sb_sample_kernel(pt_ref, bias_ref, q_ref, kn_ref, vn_ref, *rest, n_pages_step, t_new):
    k_refs = rest[:n_pages_step]
    v_refs = rest[n_pages_step:2 * n_pages_step]
    o_ref, carry_ref, acc_ref = rest[2 * n_pages_step:]
    seq = pl.program_id(0)
    g = pl.program_id(1)
    q_rows = q_ref[...]
    bias = bias_ref[...]

    def lane_tiled(x, n):
        return x[:, :n] if n <= LANES else jnp.concatenate([x] * (n // LANES), axis=1)

    def tile(s, mask, weigh_values):
        n = s.shape[1]
        z = s + lane_tiled(bias, n)
        sp = _softplus2(z)
        if mask is not None:
            sp = jnp.where(mask, sp, 0.0)
        hi, lo = _split_bf16(sp)
        c = jnp.dot(jnp.concatenate([hi, lo], axis=1), _neg_suffix_sum_weights(n),
                    preferred_element_type=F32)
        c = c + lane_tiled(carry_ref[...], n)
        a = jnp.exp2(z + c)
        if mask is not None:
            a = jnp.where(mask, a, 0.0)
        acc_ref[...] += weigh_values(a.astype(BF16))
        carry_ref[...] = jnp.broadcast_to(c[:, 0:1], carry_ref.shape)

    @pl.when(g == 0)
    def _():
        carry_ref[...] = jnp.zeros_like(carry_ref)
        acc_ref[...] = jnp.zeros_like(acc_ref)
        kn, vn = kn_ref[...], vn_ref[...]
        n = kn.shape[1]
        qry_i = lax.broadcasted_iota(jnp.int32, (LANES, n), 0) % t_new
        key_i = lax.broadcasted_iota(jnp.int32, (LANES, n), 1)
        mask = (key_i // t_new == seq) & (key_i % t_new < qry_i)
        tile(jnp.dot(q_rows, kn, preferred_element_type=F32), mask,
             lambda a: jnp.dot(a, vn, preferred_element_type=F32))

    for lo_page in reversed(range(0, n_pages_step, SAMPLE_TILE_PAGES)):
        sel = slice(lo_page, lo_page + SAMPLE_TILE_PAGES)
        kt = jnp.concatenate([r[...].astype(BF16) for r in k_refs[sel]], axis=1)
        vt = jnp.concatenate([r[...].astype(BF16) for r in v_refs[sel]], axis=1)
        tile(jnp.dot(q_rows, kt, preferred_element_type=F32), None,
             lambda a: lax.dot_general(a, vt, (((1,), (1,)), ((), ())),
                                       preferred_element_type=F32))

    @pl.when(g == pl.num_programs(1) - 1)
    def _():
        acc = acc_ref[...]
        r = lax.broadcasted_iota(jnp.int32, acc.shape, 0) // t_new
        c = lax.broadcasted_iota(jnp.int32, acc.shape, 1) // HEAD_DIM
        own = jnp.where(r == c, acc, 0.0).reshape(N_HEADS, t_new, D_MODEL)
        o_ref[...] = jnp.sum(own, axis=0).astype(o_ref.dtype)


def _sb_sample(q_rows, kt_new, v_new, cache_k, cache_v, layer, page_table, bias, *, t_new,
               pages_per_step):
    n, _, d = q_rows.shape
    n_pages = page_table.shape[1]
    assert n_pages % pages_per_step == 0
    n_steps = n_pages // pages_per_step
    n_layers, n_pool = cache_k.shape[:2]
    ck = cache_k.transpose(0, 1, 3, 4, 2).reshape(n_layers, n_pool, d, PAGE_SIZE)
    cv = cache_v.transpose(0, 1, 3, 4, 2).reshape(n_layers, n_pool, d, PAGE_SIZE)
    bias_rows = jnp.broadcast_to(jnp.repeat(bias.astype(F32), t_new)[:, None], (LANES, LANES))

    def page_spec(i):
        return pl.BlockSpec(
            (None, None, d, PAGE_SIZE),
            lambda s, g, pt: (layer, pt[s, (n_steps - 1 - g) * pages_per_step + i], 0, 0))

    whole = lambda a: pl.BlockSpec(a.shape, lambda s, g, pt: (0,) * a.ndim)
    grid_spec = pltpu.PrefetchScalarGridSpec(
        num_scalar_prefetch=1,
        grid=(n, n_steps),
        in_specs=[whole(bias_rows), pl.BlockSpec((None, LANES, d), lambda s, g, pt: (s, 0, 0)),
                  whole(kt_new), whole(v_new)]
                 + [page_spec(i) for i in range(pages_per_step)] * 2,
        out_specs=pl.BlockSpec((None, t_new, d), lambda s, g, pt: (s, 0, 0)),
        scratch_shapes=[pltpu.VMEM((LANES, LANES), F32), pltpu.VMEM((LANES, d), F32)],
    )
    return pl.pallas_call(
        functools.partial(_sb_sample_kernel, n_pages_step=pages_per_step, t_new=t_new),
        grid_spec=grid_spec,
        out_shape=jax.ShapeDtypeStruct((n, t_new, d), BF16),
        compiler_params=pltpu.CompilerParams(
            dimension_semantics=("parallel", "arbitrary"), vmem_limit_bytes=VMEM_LIMIT),
        name="sb_sample",
    )(page_table, bias_rows, q_rows, kt_new, v_new,
      *([ck] * pages_per_step), *([cv] * pages_per_step))


def _cm_kernel(x_ref, wuv_ref, g_ref, b_ref, ws_ref, bias_ref, gate_ref, *rest, chunk, with_v):
    rows = ws_ref.shape[-1]
    xb = x_ref[...].astype(BF16)
    z = jnp.dot(xb, wuv_ref[...], preferred_element_type=F32)
    z = 0.5 * z * (1.0 + lax.erf(z * (0.5 ** 0.5)))
    u = z[:, :CM_WIDTH]
    v = _layer_norm(z[:, CM_WIDTH:], g_ref[...], b_ref[...])
    if with_v:
        rest[0][...] = v
    r = lax.broadcasted_iota(jnp.int32, (rows, rows), 0)
    c = lax.broadcasted_iota(jnp.int32, (rows, rows), 1)
    keep = c <= r
    if chunk < rows:
        keep &= (r // chunk) == (c // chunk)
    w_sp = [jnp.where(keep, ws_ref[g], 0.0).astype(BF16) for g in range(CM_GROUPS)]
    vb = v.astype(BF16)
    bias = bias_ref[...]
    for ci in range(x_ref.shape[0] // rows):
        rs = slice(ci * rows, (ci + 1) * rows)
        for g in range(CM_GROUPS):
            cs = slice(g * CM_GROUP_DIM, (g + 1) * CM_GROUP_DIM)
            mixed = jnp.dot(w_sp[g], vb[rs, cs], preferred_element_type=F32) + bias[:, cs]
            gate_ref[rs, cs] = (u[rs, cs] * mixed).astype(gate_ref.dtype)


def _chunk_mlp(x, w_uv_bf, ln_g, ln_b, layer, w_sp, bias_full, *, tm, chunk, with_v):
    n, d = x.shape
    rows = w_sp.shape[-1]
    assert n % tm == 0 and tm % rows == 0 and rows % chunk == 0
    row_spec = pl.BlockSpec((tm, d), lambda i: (i, 0))
    out_shape = [jax.ShapeDtypeStruct((n, CM_WIDTH), BF16)]
    out_specs = [pl.BlockSpec((tm, CM_WIDTH), lambda i: (i, 0))]
    if with_v:
        out_shape.append(jax.ShapeDtypeStruct((n, CM_WIDTH), F32))
        out_specs.append(pl.BlockSpec((tm, CM_WIDTH), lambda i: (i, 0)))
    return pl.pallas_call(
        functools.partial(_cm_kernel, chunk=chunk, with_v=with_v),
        grid=(n // tm,),
        in_specs=[row_spec, _layer_spec(w_uv_bf, layer), _layer_spec(ln_g, layer),
                  _layer_spec(ln_b, layer), _const_spec(w_sp.shape), _const_spec(bias_full.shape)],
        out_specs=out_specs,
        out_shape=out_shape,
        compiler_params=pltpu.CompilerParams(
            dimension_semantics=("parallel",), vmem_limit_bytes=VMEM_LIMIT),
        name="chunk_mlp",
    )(x, w_uv_bf, ln_g, ln_b, w_sp, bias_full)


def _tail_kernel(x_ref, pre_ref, p_ref, wo_ref, g1_ref, b1_ref, win_ref, wout_ref,
                 g2_ref, b2_ref, wg_ref, bg_ref, wp_ref, o_ref):
    mix = jnp.dot(pre_ref[...], wo_ref[...], preferred_element_type=F32)
    h = _layer_norm(DEEPNORM_ALPHA * x_ref[...] + mix, g1_ref[...], b1_ref[...])
    gu = jnp.dot(h.astype(BF16), win_ref[...], preferred_element_type=F32)
    gate_half = gu[:, :D_FF]
    act = gate_half * jax.nn.sigmoid(gate_half) * gu[:, D_FF:]
    f = jnp.dot(act.astype(BF16), wout_ref[...], preferred_element_type=F32)
    h = _layer_norm(DEEPNORM_ALPHA * h + f, g2_ref[...], b2_ref[...])
    ple_gate = jax.nn.sigmoid(
        jnp.dot(h.astype(BF16), wg_ref[...], preferred_element_type=F32) + bg_ref[...])
    ple = jnp.dot(p_ref[...].astype(BF16), wp_ref[...], preferred_element_type=F32)
    o_ref[...] = h + ple_gate * ple


def _layer_tail(x, pre_bf, p, layer, wo, mixer_layer, params, *, tm):
    n, d = x.shape
    assert n % tm == 0
    row = lambda w: pl.BlockSpec((tm, w), lambda i: (i, 0))
    return pl.pallas_call(
        _tail_kernel,
        grid=(n // tm,),
        in_specs=[row(d), row(d), pl.BlockSpec((None, tm, PLE_DIM), lambda i: (layer, i, 0)),
                  _layer_spec(wo, mixer_layer)] + [_layer_spec(a, layer) for a in params],
        out_specs=row(d),
        out_shape=jax.ShapeDtypeStruct((n, d), F32),
        compiler_params=pltpu.CompilerParams(
            dimension_semantics=("parallel",), vmem_limit_bytes=VMEM_LIMIT),
        name="layer_tail",
    )(x, pre_bf, p, wo, *params)


def _tile_sizes(n_rows, preferred):
    t = min(preferred, n_rows)
    while n_rows % t:
        t -= 8
    return t


def kernel(x_prompt, x_sample, cache_k, cache_v, page_table, p_prompt, p_sample, w_qkv, w_o_sb,
           sb_bias, w_uv, ln_v_g, ln_v_b, w_s, b_s, w_o_cm, ln1_g, ln1_b, ln2_g, ln2_b,
           w_ffn_in, w_ffn_out, w_ple_gate, b_ple_gate, w_ple_proj):
    b, s, d = x_prompt.shape
    n_seq, t_new, _ = x_sample.shape
    n_p, n_s = b * s, n_seq * t_new
    depth = w_ffn_in.shape[0]
    attn_tile = 256
    attn_tq = _tile_sizes(s, 1024)
    tm_p = _tile_sizes(s, 512)
    pages_per_step = next(p for p in (8, 4, 2, 1) if page_table.shape[1] % p == 0)

    vec = lambda a: a.reshape(a.shape[0], 1, a.shape[1])
    w_qkv_bf, w_uv_bf = w_qkv.astype(BF16), w_uv.astype(BF16)
    w_o_sb_bf, w_o_cm_bf = w_o_sb.astype(BF16), w_o_cm.astype(BF16)
    ln_v_g3, ln_v_b3 = vec(ln_v_g), vec(ln_v_b)
    tail_params = (vec(ln1_g), vec(ln1_b), w_ffn_in.astype(BF16), w_ffn_out.astype(BF16),
                   vec(ln2_g), vec(ln2_b), w_ple_gate.astype(BF16), vec(b_ple_gate),
                   w_ple_proj.astype(BF16))
    p_prompt2 = p_prompt.reshape(depth, n_p, PLE_DIM)
    p_sample2 = p_sample.reshape(depth, n_s, PLE_DIM)

    xp = x_prompt
    xs = x_sample.reshape(n_s, d)
    kp_list, vp_list, ks_list, vs_list, cv_list = [], [], [], [], []
    for i in range(depth):
        j = i // 2
        if i % 2 == 0:
            q_bf, v_bf, kt, vt, kt_bf = _qkv_prompt(xp, w_qkv_bf, j, tm=tm_p, tk=attn_tile)
            bias2 = sb_bias[j] * LOG2E
            pre_p = _sb_prompt(q_bf, kt_bf, v_bf, bias2, tq=attn_tq).reshape(n_p, d)
            heads_last = lambda a: a.reshape(b, N_HEADS, HEAD_DIM, s).transpose(0, 3, 1, 2)
            kp_list.append(heads_last(kt))
            vp_list.append(heads_last(vt))

            q_rows, kn, vn, knt_bf, vn_bf = _qkv_sample(xs, w_qkv_bf, j, t_new=t_new)
            pre_s = _sb_sample(q_rows, knt_bf, vn_bf, cache_k, cache_v, j, page_table, bias2,
                               t_new=t_new, pages_per_step=pages_per_step).reshape(n_s, d)
            ks_list.append(kn.reshape(n_seq, t_new, N_HEADS, HEAD_DIM))
            vs_list.append(vn.reshape(n_seq, t_new, N_HEADS, HEAD_DIM))
            w_o = w_o_sb_bf
        else:
            bias_p = jnp.repeat(b_s[j].T, CM_GROUP_DIM, axis=1)
            pre_p, = _chunk_mlp(xp.reshape(n_p, d), w_uv_bf, ln_v_g3, ln_v_b3, j, w_s[j], bias_p,
                                tm=tm_p, chunk=CHUNK, with_v=False)
            w_rep = jnp.tile(w_s[j][:, :t_new, :t_new], (1, n_seq, n_seq))
            bias_s = jnp.tile(jnp.repeat(b_s[j].T[:t_new], CM_GROUP_DIM, axis=1), (n_seq, 1))
            pre_s, v_s = _chunk_mlp(xs, w_uv_bf, ln_v_g3, ln_v_b3, j, w_rep, bias_s,
                                    tm=n_s, chunk=t_new, with_v=True)
            cv_list.append(v_s.reshape(n_seq, t_new, CM_WIDTH))
            w_o = w_o_cm_bf

        xp = _layer_tail(xp.reshape(n_p, d), pre_p, p_prompt2, i, w_o, j, tail_params,
                         tm=_tile_sizes(n_p, 512)).reshape(b, s, d)
        xs = _layer_tail(xs, pre_s, p_sample2, i, w_o, j, tail_params, tm=_tile_sizes(n_s, 256))
    return (xp, xs.reshape(n_seq, t_new, d), jnp.stack(kp_list), jnp.stack(vp_list),
            jnp.stack(ks_list), jnp.stack(vs_list), jnp.stack(cv_list))
```

```python
import functools

import jax
import jax.numpy as jnp
from jax import lax
from jax.experimental import pallas as pl
from jax.experimental.pallas import tpu as pltpu

D_MODEL = 1024
N_HEADS = 16
HEAD_DIM = 64
PAGE_SIZE = 128
CHUNK = 128
CM_WIDTH = D_MODEL
CM_GROUPS = 8
CM_GROUP_DIM = CM_WIDTH // CM_GROUPS
D_FF = 2816
PLE_DIM = 256
DEPTH = 4
DEEPNORM_ALPHA = (2 * DEPTH) ** 0.25
LN_EPS = 1e-5
LOG2E = 1.4426950408889634

LANES = 128
MXU_TILE = 256
ROW_TILE = 512
ATTN_Q_ROWS = 1024
NORM_SLACK = 1.01
HEADS_PER_BLOCK = LANES // HEAD_DIM
N_HEAD_BLOCKS = N_HEADS // HEADS_PER_BLOCK
VMEM_LIMIT = 56 * 1024 * 1024
SAMPLE_TILE_PAGES = 4
EXIT_LOG2 = 160.0
EXP2_CAP = 120.0

F32 = jnp.float32
BF16 = jnp.bfloat16


def _const_spec(shape):
    nd = len(shape)
    return pl.BlockSpec(shape, lambda *_: (0,) * nd, pipeline_mode=pl.Buffered(1))


def _layer_spec(stacked, layer):
    nd = stacked.ndim
    return pl.BlockSpec((None,) + stacked.shape[1:], lambda *_: (layer,) + (0,) * (nd - 1),
                        pipeline_mode=pl.Buffered(1))


def _layer_norm(x, g, b):
    mu = jnp.mean(x, axis=-1, keepdims=True)
    xc = x - mu
    var = jnp.mean(xc * xc, axis=-1, keepdims=True)
    return xc * lax.rsqrt(var + LN_EPS) * g + b


def _softplus2(z2):
    return jnp.maximum(jnp.log2(1.0 + jnp.exp2(jnp.minimum(z2, EXP2_CAP))), z2)


def _split_bf16(x):
    hi = x.astype(BF16)
    lo = (x - hi.astype(F32)).astype(BF16)
    return hi, lo


def _neg_suffix_sum_weights(n):
    r = lax.broadcasted_iota(jnp.int32, (n, n), 0)
    c = lax.broadcasted_iota(jnp.int32, (n, n), 1)
    tri = jnp.where(r >= c, -1.0, 0.0).astype(BF16)
    return jnp.concatenate([tri, tri], axis=0)


Q_SCALE = HEAD_DIM ** -0.5 * LOG2E


def _qkv_prompt_kernel(x_ref, w_ref, *rest, tk, first):
    q_ref, vb_ref, kt_ref, vt_ref, ktb_ref = rest if first else rest[2:]
    xb = x_ref[...].astype(BF16)
    qkv = jnp.dot(xb, w_ref[...], preferred_element_type=F32)
    k = qkv[:, D_MODEL:2 * D_MODEL]
    v = qkv[:, 2 * D_MODEL:]
    q_ref[...] = (qkv[:, :D_MODEL] * Q_SCALE).astype(BF16)
    vb_ref[...] = v.astype(BF16)
    kt, vt = k.T, v.T
    if first:
        for l in range(kt_ref.shape[0]):
            kt_ref[l] = kt
            vt_ref[l] = vt
    else:
        kt_ref[...] = kt
        vt_ref[...] = vt
    for j in range(ktb_ref.shape[0]):
        ktb_ref[j] = kt[:, j * tk:(j + 1) * tk].astype(BF16)


def _qkv_prompt(x, w_bf, layer, kt_all, vt_all, *, tm, tk):
    b, s, d = x.shape
    n_layers = w_bf.shape[0]
    first = kt_all is None
    assert s % tm == 0 and tm % tk == 0 and first == (layer == 0)
    row_spec = pl.BlockSpec((None, tm, d), lambda bi, i: (bi, i, 0))
    if first:
        stack_spec = pl.BlockSpec((n_layers, None, d, tm), lambda bi, i: (0, bi, 0, i))
        stacks, stack_in_specs, aliases = (), [], {}
    else:
        stack_spec = pl.BlockSpec((None, None, d, tm), lambda bi, i: (layer, bi, 0, i))
        stacks, stack_in_specs = (kt_all, vt_all), [pl.BlockSpec(memory_space=pl.ANY)] * 2
        aliases = {2: 2, 3: 3}
    stack_shape = jax.ShapeDtypeStruct((n_layers, b, d, s), F32)
    return pl.pallas_call(
        functools.partial(_qkv_prompt_kernel, tk=tk, first=first),
        grid=(b, s // tm),
        in_specs=[row_spec, _layer_spec(w_bf, layer)] + stack_in_specs,
        out_specs=[row_spec, row_spec, stack_spec, stack_spec,
                   pl.BlockSpec((None, tm // tk, d, tk), lambda bi, i: (bi, i, 0, 0))],
        out_shape=[jax.ShapeDtypeStruct((b, s, d), BF16), jax.ShapeDtypeStruct((b, s, d), BF16),
                   stack_shape, stack_shape, jax.ShapeDtypeStruct((b, s // tk, d, tk), BF16)],
        input_output_aliases=aliases,
        compiler_params=pltpu.CompilerParams(
            dimension_semantics=("parallel", "parallel"), vmem_limit_bytes=VMEM_LIMIT),
        name="qkv_prompt",
    )(x, w_bf, *stacks)


def _qkv_sample_kernel(x_ref, w_ref, qrows_ref, k_ref, v_ref, ktb_ref, vb_ref, *, t_new):
    n_s = x_ref.shape[0]
    n_seq = n_s // t_new
    xb = x_ref[...].astype(BF16)
    qkv = jnp.dot(xb, w_ref[...], preferred_element_type=F32)
    k = qkv[:, D_MODEL:2 * D_MODEL]
    v = qkv[:, 2 * D_MODEL:]
    k_ref[...] = k
    v_ref[...] = v
    ktb_ref[...] = k.T.astype(BF16)
    vb_ref[...] = v.astype(BF16)
    q4 = (qkv[:, :D_MODEL] * Q_SCALE).reshape(n_seq, 1, t_new, D_MODEL)
    rows = jnp.broadcast_to(q4, (n_seq, N_HEADS, t_new, D_MODEL)).reshape(n_seq, LANES, D_MODEL)
    r = lax.broadcasted_iota(jnp.int32, (n_seq, LANES, D_MODEL), 1) // t_new
    c = lax.broadcasted_iota(jnp.int32, (n_seq, LANES, D_MODEL), 2) // HEAD_DIM
    qrows_ref[...] = jnp.where(r == c, rows, 0.0).astype(BF16)


def _qkv_sample(x, w_bf, layer, *, t_new):
    n_s, d = x.shape
    assert N_HEADS * t_new == LANES
    whole = lambda shape: pl.BlockSpec(shape, lambda i: (0,) * len(shape))
    out_shape = [jax.ShapeDtypeStruct((n_s // t_new, LANES, d), BF16),
                 jax.ShapeDtypeStruct((n_s, d), F32), jax.ShapeDtypeStruct((n_s, d), F32),
                 jax.ShapeDtypeStruct((d, n_s), BF16), jax.ShapeDtypeStruct((n_s, d), BF16)]
    return pl.pallas_call(
        functools.partial(_qkv_sample_kernel, t_new=t_new),
        grid=(1,),
        in_specs=[whole((n_s, d)), _layer_spec(w_bf, layer)],
        out_specs=[whole(o.shape) for o in out_shape],
        out_shape=out_shape,
        compiler_params=pltpu.CompilerParams(
            dimension_semantics=("arbitrary",), vmem_limit_bytes=VMEM_LIMIT),
        name="qkv_sample",
    )(x, w_bf)


def _sb_prompt_kernel(bias_ref, q_ref, kt_ref, v_ref, o_ref, knorm_ref, *, tq, tk):
    hb = pl.program_id(1)
    qi = pl.program_id(2)
    heads = range(HEADS_PER_BLOCK)
    n_band = tq // tk
    tiles_per_trip = n_band
    q2 = q_ref[...]
    qlane_i = lax.broadcasted_iota(jnp.int32, q2.shape, 1)
    qlane = qlane_i // HEAD_DIM
    vlane = lax.broadcasted_iota(jnp.int32, (tk, LANES), 1) // HEAD_DIM
    krow_i = lax.broadcasted_iota(jnp.int32, (LANES, tk), 0)
    biases = [bias_ref[hb * HEADS_PER_BLOCK + h] for h in heads]
    w_cum = _neg_suffix_sum_weights(tk)[:tk]
    q_heads, k_own_rows, k_bias_rows = [], [], []
    for h in heads:
        spare = (HEADS_PER_BLOCK - 1 - h) * HEAD_DIM
        ones = ((qlane_i == spare) | (qlane_i == spare + 1)).astype(BF16)
        q_heads.append(jnp.where(qlane == h, q2, ones))
        b_full = jnp.zeros((LANES, tk), F32) + biases[h]
        b_hi = b_full.astype(BF16)
        b_lo = (b_full - b_hi.astype(F32)).astype(BF16)
        k_own_rows.append(jnp.where(krow_i // HEAD_DIM == h, 1.0, 0.0).astype(BF16))
        k_bias_rows.append(jnp.where(krow_i == spare, b_full,
                                     jnp.where(krow_i == spare + 1, b_full - b_hi.astype(F32), 0.0)
                                     ).astype(BF16))

    def tile(kb, carry, r0):
        carries, acc = carry
        masked = r0 is not None
        r0 = r0 or 0
        rows = tq - r0
        keep_head_rows = lambda old, new: jnp.concatenate([old[:r0], new], axis=0) if r0 else new
        kt2 = kt_ref[kb]
        v2 = v_ref[pl.ds(pl.multiple_of(kb * tk, tk), tk), :]
        if masked:
            causal = (lax.broadcasted_iota(jnp.int32, (rows, tk), 1)
                      < lax.broadcasted_iota(jnp.int32, (rows, tk), 0))
        new_carries, a_parts = [], []
        for h in heads:
            kt_h = kt2 * k_own_rows[h] + k_bias_rows[h]
            z = jnp.dot(q_heads[h][r0:], kt_h, preferred_element_type=F32)
            sp = _softplus2(z)
            if masked:
                sp = jnp.where(causal, sp, 0.0)
            c = jnp.dot(sp.astype(BF16), w_cum,
                        preferred_element_type=F32) + carries[h][r0:]
            a = jnp.exp2(z + c)
            if masked:
                a = jnp.where(causal, a, 0.0)
            a_parts.append(a.astype(BF16))
            new_carries.append(keep_head_rows(carries[h], c[:, 0:1]))
        v_st = jnp.concatenate([jnp.where(vlane == h, v2, jnp.zeros_like(v2)) for h in heads], axis=0)
        upd = acc[r0:] + jnp.dot(jnp.concatenate(a_parts, axis=1), v_st, preferred_element_type=F32)
        return tuple(new_carries), keep_head_rows(acc, upd)

    carry = (tuple(jnp.zeros((tq, 1), F32) for _ in heads), jnp.zeros((tq, LANES), F32))
    first_kb = qi * n_band
    for j in reversed(range(n_band)):
        carry = tile(first_kb + j, carry, j * tk)

    @pl.when(qi == 0)
    def _():
        kt_all = kt_ref[...].astype(F32)
        k_sq = kt_all * kt_all
        for h in heads:
            per_key = jnp.sum(k_sq[:, h * HEAD_DIM:(h + 1) * HEAD_DIM, :], axis=1)
            knorm_ref[h] = jnp.max(per_key)

    q_sq = q2.astype(F32) * q2.astype(F32)
    z_max = []
    for h in heads:
        q_norm2 = jnp.sum(jnp.where(qlane == h, q_sq, 0.0), axis=1, keepdims=True)
        z_max.append(jnp.sqrt(q_norm2 * knorm_ref[h]) * NORM_SLACK + jnp.abs(jnp.zeros((tq, 1), F32)
                                                                     + biases[h]))

    def log2_weight_bound(carries):
        return jnp.max(jnp.maximum(*(carries[h] + z_max[h] for h in heads)))

    def more_to_do(state):
        t, bound, _ = state
        return (t < first_kb // tiles_per_trip) & (bound > -EXIT_LOG2)

    def past_tiles(state):
        t, _, cr = state
        for j in range(tiles_per_trip):
            cr = tile(first_kb - 1 - (t * tiles_per_trip + j), cr, None)
        return t + 1, log2_weight_bound(cr[0]), cr

    _, _, carry = lax.while_loop(more_to_do, past_tiles,
                                 (jnp.int32(0), log2_weight_bound(carry[0]), carry))
    o_ref[...] = carry[1].astype(o_ref.dtype)


def _sb_prompt(q_bf, kt_bf, v_bf, bias, *, tq):
    b, s, d = q_bf.shape
    tk = kt_bf.shape[-1]
    assert tq % tk == 0 and s % tq == 0
    grid = (b, N_HEAD_BLOCKS, s // tq)
    return pl.pallas_call(
        functools.partial(_sb_prompt_kernel, tq=tq, tk=tk),
        grid=grid,
        in_specs=[
            pl.BlockSpec(memory_space=pltpu.SMEM),
            pl.BlockSpec((None, tq, LANES), lambda bi, hb, i: (bi, i, hb)),
            pl.BlockSpec((None, s // tk, LANES, tk), lambda bi, hb, i: (bi, 0, hb, 0)),
            pl.BlockSpec((None, s, LANES), lambda bi, hb, i: (bi, 0, hb)),
        ],
        out_specs=pl.BlockSpec((None, tq, LANES), lambda bi, hb, i: (bi, i, hb)),
        out_shape=jax.ShapeDtypeStruct((b, s, d), BF16),
        scratch_shapes=[pltpu.SMEM((HEADS_PER_BLOCK,), F32)],
        compiler_params=pltpu.CompilerParams(
            dimension_semantics=("parallel", "parallel", "arbitrary"),
            vmem_limit_bytes=VMEM_LIMIT),
        name="sb_prompt",
    )(bias, q_bf, kt_bf, v_bf)


def _sb_sample_kernel(pt_ref, bias_ref, q_ref, kn_ref, vn_ref, *rest, n_pages_step, t_new):
    k_refs = rest[:n_pages_step]
    v_refs = rest[n_pages_step:2 * n_pages_step]
    o_ref, carry_ref, acc_ref = rest[2 * n_pages_step:]
    seq = pl.program_id(0)
    g = pl.program_id(1)
    q_rows = q_ref[...]
    bias = bias_ref[...]

    def lane_tiled(x, n):
        return x[:, :n] if n <= LANES else jnp.concatenate([x] * (n // LANES), axis=1)

    def tile(s, mask, weigh_values):
        n = s.shape[1]
        z = s + lane_tiled(bias, n)
        sp = _softplus2(z)
        if mask is not None:
            sp = jnp.where(mask, sp, 0.0)
        hi, lo = _split_bf16(sp)
        c = jnp.dot(jnp.concatenate([hi, lo], axis=1), _neg_suffix_sum_weights(n),
                    preferred_element_type=F32)
        c = c + lane_tiled(carry_ref[...], n)
        a = jnp.exp2(z + c)
        if mask is not None:
            a = jnp.where(mask, a, 0.0)
        acc_ref[...] += weigh_values(a.astype(BF16))
        carry_ref[...] = jnp.broadcast_to(c[:, 0:1], carry_ref.shape)

    @pl.when(g == 0)
    def _():
        carry_ref[...] = jnp.zeros_like(carry_ref)
        acc_ref[...] = jnp.zeros_like(acc_ref)
        kn, vn = kn_ref[...], vn_ref[...]
        n = kn.shape[1]
        qry_i = lax.broadcasted_iota(jnp.int32, (LANES, n), 0) % t_new
        key_i = lax.broadcasted_iota(jnp.int32, (LANES, n), 1)
        mask = (key_i // t_new == seq) & (key_i % t_new < qry_i)
        tile(jnp.dot(q_rows, kn, preferred_element_type=F32), mask,
             lambda a: jnp.dot(a, vn, preferred_element_type=F32))

    for lo_page in reversed(range(0, n_pages_step, SAMPLE_TILE_PAGES)):
        sel = slice(lo_page, lo_page + SAMPLE_TILE_PAGES)
        kt = jnp.concatenate([r[...].astype(BF16) for r in k_refs[sel]], axis=1)
        vt = jnp.concatenate([r[...].astype(BF16) for r in v_refs[sel]], axis=1)
        tile(jnp.dot(q_rows, kt, preferred_element_type=F32), None,
             lambda a: lax.dot_general(a, vt, (((1,), (1,)), ((), ())),
                                       preferred_element_type=F32))

    @pl.when(g == pl.num_programs(1) - 1)
    def _():
        acc = acc_ref[...]
        r = lax.broadcasted_iota(jnp.int32, acc.shape, 0) // t_new
        c = lax.broadcasted_iota(jnp.int32, acc.shape, 1) // HEAD_DIM
        own = jnp.where(r == c, acc, 0.0).reshape(N_HEADS, t_new, D_MODEL)
        o_ref[...] = jnp.sum(own, axis=0).astype(o_ref.dtype)


def _sb_sample(q_rows, kt_new, v_new, cache_k, cache_v, layer, page_table, bias, *, t_new,
               pages_per_step):
    n, _, d = q_rows.shape
    n_pages = page_table.shape[1]
    assert n_pages % pages_per_step == 0
    n_steps = n_pages // pages_per_step
    n_layers, n_pool = cache_k.shape[:2]
    ck = cache_k.transpose(0, 1, 3, 4, 2).reshape(n_layers, n_pool, d, PAGE_SIZE)
    cv = cache_v.transpose(0, 1, 3, 4, 2).reshape(n_layers, n_pool, d, PAGE_SIZE)
    bias_rows = jnp.broadcast_to(jnp.repeat(bias.astype(F32), t_new)[:, None], (LANES, LANES))

    def page_spec(i):
        return pl.BlockSpec(
            (None, None, d, PAGE_SIZE),
            lambda s, g, pt: (layer, pt[s, (n_steps - 1 - g) * pages_per_step + i], 0, 0))

    whole = lambda a: pl.BlockSpec(a.shape, lambda s, g, pt: (0,) * a.ndim)
    grid_spec = pltpu.PrefetchScalarGridSpec(
        num_scalar_prefetch=1,
        grid=(n, n_steps),
        in_specs=[whole(bias_rows), pl.BlockSpec((None, LANES, d), lambda s, g, pt: (s, 0, 0)),
                  whole(kt_new), whole(v_new)]
                 + [page_spec(i) for i in range(pages_per_step)] * 2,
        out_specs=pl.BlockSpec((None, t_new, d), lambda s, g, pt: (s, 0, 0)),
        scratch_shapes=[pltpu.VMEM((LANES, LANES), F32), pltpu.VMEM((LANES, d), F32)],
    )
    return pl.pallas_call(
        functools.partial(_sb_sample_kernel, n_pages_step=pages_per_step, t_new=t_new),
        grid_spec=grid_spec,
        out_shape=jax.ShapeDtypeStruct((n, t_new, d), BF16),
        compiler_params=pltpu.CompilerParams(
            dimension_semantics=("parallel", "arbitrary"), vmem_limit_bytes=VMEM_LIMIT),
        name="sb_sample",
    )(page_table, bias_rows, q_rows, kt_new, v_new,
      *([ck] * pages_per_step), *([cv] * pages_per_step))


def _cm_kernel(x_ref, wuv_ref, g_ref, b_ref, ws_ref, bias_ref, gate_ref, *rest, chunk, with_v):
    rows = ws_ref.shape[-1]
    xb = x_ref[...].astype(BF16)
    z = jnp.dot(xb, wuv_ref[...], preferred_element_type=F32)
    z = 0.5 * z * (1.0 + lax.erf(z * (0.5 ** 0.5)))
    u = z[:, :CM_WIDTH]
    v = _layer_norm(z[:, CM_WIDTH:], g_ref[...], b_ref[...])
    if with_v:
        rest[0][...] = v
    r = lax.broadcasted_iota(jnp.int32, (rows, rows), 0)
    c = lax.broadcasted_iota(jnp.int32, (rows, rows), 1)
    keep = c <= r
    if chunk < rows:
        keep &= (r // chunk) == (c // chunk)
    w_sp = [jnp.where(keep, ws_ref[g], 0.0).astype(BF16) for g in range(CM_GROUPS)]
    vb = v.astype(BF16)
    bias = bias_ref[...]
    for ci in range(x_ref.shape[0] // rows):
        rs = slice(ci * rows, (ci + 1) * rows)
        for g in range(CM_GROUPS):
            cs = slice(g * CM_GROUP_DIM, (g + 1) * CM_GROUP_DIM)
            mixed = jnp.dot(w_sp[g], vb[rs, cs], preferred_element_type=F32) + bias[:, cs]
            gate_ref[rs, cs] = (u[rs, cs] * mixed).astype(gate_ref.dtype)


def _chunk_mlp(x, w_uv_bf, ln_g, ln_b, layer, w_sp, bias_full, *, tm, chunk, with_v):
    n, d = x.shape
    rows = w_sp.shape[-1]
    assert n % tm == 0 and tm % rows == 0 and rows % chunk == 0
    row_spec = pl.BlockSpec((tm, d), lambda i: (i, 0))
    out_shape = [jax.ShapeDtypeStruct((n, CM_WIDTH), BF16)]
    out_specs = [pl.BlockSpec((tm, CM_WIDTH), lambda i: (i, 0))]
    if with_v:
        out_shape.append(jax.ShapeDtypeStruct((n, CM_WIDTH), F32))
        out_specs.append(pl.BlockSpec((tm, CM_WIDTH), lambda i: (i, 0)))
    return pl.pallas_call(
        functools.partial(_cm_kernel, chunk=chunk, with_v=with_v),
        grid=(n // tm,),
        in_specs=[row_spec, _layer_spec(w_uv_bf, layer), _layer_spec(ln_g, layer),
                  _layer_spec(ln_b, layer), _const_spec(w_sp.shape), _const_spec(bias_full.shape)],
        out_specs=out_specs,
        out_shape=out_shape,
        compiler_params=pltpu.CompilerParams(
            dimension_semantics=("parallel",), vmem_limit_bytes=VMEM_LIMIT),
        name="chunk_mlp",
    )(x, w_uv_bf, ln_g, ln_b, w_sp, bias_full)


def _tail_kernel(x_ref, pre_ref, p_ref, wo_ref, g1_ref, b1_ref, win_ref, wout_ref,
                 g2_ref, b2_ref, wg_ref, bg_ref, wp_ref, o_ref):
    mix = jnp.dot(pre_ref[...], wo_ref[...], preferred_element_type=F32)
    h = _layer_norm(DEEPNORM_ALPHA * x_ref[...] + mix, g1_ref[...], b1_ref[...])
    gu = jnp.dot(h.astype(BF16), win_ref[...], preferred_element_type=F32)
    gate_half = gu[:, :D_FF]
    act = gate_half * jax.nn.sigmoid(gate_half) * gu[:, D_FF:]
    f = jnp.dot(act.astype(BF16), wout_ref[...], preferred_element_type=F32)
    h = _layer_norm(DEEPNORM_ALPHA * h + f, g2_ref[...], b2_ref[...])
    ple_gate = jax.nn.sigmoid(
        jnp.dot(h.astype(BF16), wg_ref[...], preferred_element_type=F32) + bg_ref[...])
    ple = jnp.dot(p_ref[...].astype(BF16), wp_ref[...], preferred_element_type=F32)
    o_ref[...] = h + ple_gate * ple


def _layer_tail(x, pre_bf, p, layer, wo, mixer_layer, params, *, tm):
    n, d = x.shape
    assert n % tm == 0
    row = lambda w: pl.BlockSpec((tm, w), lambda i: (i, 0))
    return pl.pallas_call(
        _tail_kernel,
        grid=(n // tm,),
        in_specs=[row(d), row(d), pl.BlockSpec((None, tm, PLE_DIM), lambda i: (layer, i, 0)),
                  _layer_spec(wo, mixer_layer)] + [_layer_spec(a, layer) for a in params],
        out_specs=row(d),
        out_shape=jax.ShapeDtypeStruct((n, d), F32),
        compiler_params=pltpu.CompilerParams(
            dimension_semantics=("parallel",), vmem_limit_bytes=VMEM_LIMIT),
        name="layer_tail",
    )(x, pre_bf, p, wo, *params)


def _tile_sizes(n_rows, preferred):
    t = min(preferred, n_rows)
    while n_rows % t:
        t -= 8
    return t


def kernel(x_prompt, x_sample, cache_k, cache_v, page_table, p_prompt, p_sample, w_qkv, w_o_sb,
           sb_bias, w_uv, ln_v_g, ln_v_b, w_s, b_s, w_o_cm, ln1_g, ln1_b, ln2_g, ln2_b,
           w_ffn_in, w_ffn_out, w_ple_gate, b_ple_gate, w_ple_proj):
    b, s, d = x_prompt.shape
    n_seq, t_new, _ = x_sample.shape
    n_p, n_s = b * s, n_seq * t_new
    depth = w_ffn_in.shape[0]
    assert depth == DEPTH and s % CHUNK == 0
    attn_tile = MXU_TILE
    attn_tq = _tile_sizes(s, ATTN_Q_ROWS)
    tm_p = _tile_sizes(s, ROW_TILE)
    pages_per_step = next(p for p in (8, 4, 2, 1) if page_table.shape[1] % p == 0)

    vec = lambda a: a.reshape(a.shape[0], 1, a.shape[1])
    w_qkv_bf, w_uv_bf = w_qkv.astype(BF16), w_uv.astype(BF16)
    w_o_sb_bf, w_o_cm_bf = w_o_sb.astype(BF16), w_o_cm.astype(BF16)
    ln_v_g3, ln_v_b3 = vec(ln_v_g), vec(ln_v_b)
    tail_params = (vec(ln1_g), vec(ln1_b), w_ffn_in.astype(BF16), w_ffn_out.astype(BF16),
                   vec(ln2_g), vec(ln2_b), w_ple_gate.astype(BF16), vec(b_ple_gate),
                   w_ple_proj.astype(BF16))
    p_prompt2 = p_prompt.reshape(depth, n_p, PLE_DIM)
    p_sample2 = p_sample.reshape(depth, n_s, PLE_DIM)

    xp = x_prompt
    xs = x_sample.reshape(n_s, d)
    kt_all = vt_all = None
    ks_list, vs_list, cv_list = [], [], []
    for i in range(depth):
        j = i // 2
        if i % 2 == 0:
            q_bf, v_bf, kt_all, vt_all, kt_bf = _qkv_prompt(xp, w_qkv_bf, j, kt_all, vt_all,
                                                             tm=tm_p, tk=attn_tile)
            bias2 = sb_bias[j] * LOG2E
            pre_p = _sb_prompt(q_bf, kt_bf, v_bf, bias2, tq=attn_tq).reshape(n_p, d)

            q_rows, kn, vn, knt_bf, vn_bf = _qkv_sample(xs, w_qkv_bf, j, t_new=t_new)
            pre_s = _sb_sample(q_rows, knt_bf, vn_bf, cache_k, cache_v, j, page_table, bias2,
                               t_new=t_new, pages_per_step=pages_per_step).reshape(n_s, d)
            ks_list.append(kn.reshape(n_seq, t_new, N_HEADS, HEAD_DIM))
            vs_list.append(vn.reshape(n_seq, t_new, N_HEADS, HEAD_DIM))
            w_o = w_o_sb_bf
        else:
            bias_p = jnp.repeat(b_s[j].T, CM_GROUP_DIM, axis=1)
            pre_p, = _chunk_mlp(xp.reshape(n_p, d), w_uv_bf, ln_v_g3, ln_v_b3, j, w_s[j], bias_p,
                                tm=tm_p, chunk=CHUNK, with_v=False)
            w_rep = jnp.tile(w_s[j][:, :t_new, :t_new], (1, n_seq, n_seq))
            bias_s = jnp.tile(jnp.repeat(b_s[j].T[:t_new], CM_GROUP_DIM, axis=1), (n_seq, 1))
            pre_s, v_s = _chunk_mlp(xs, w_uv_bf, ln_v_g3, ln_v_b3, j, w_rep, bias_s,
                                    tm=n_s, chunk=t_new, with_v=True)
            cv_list.append(v_s.reshape(n_seq, t_new, CM_WIDTH))
            w_o = w_o_cm_bf

        xp = _layer_tail(xp.reshape(n_p, d), pre_p, p_prompt2, i, w_o, j, tail_params,
                         tm=_tile_sizes(n_p, ROW_TILE)).reshape(b, s, d)
        xs = _layer_tail(xs, pre_s, p_sample2, i, w_o, j, tail_params,
                         tm=_tile_sizes(n_s, ROW_TILE))
    heads_last = lambda a: a.reshape(-1, b, N_HEADS, HEAD_DIM, s).transpose(0, 1, 4, 2, 3)
    return (xp, xs.reshape(n_seq, t_new, d), heads_last(kt_all), heads_last(vt_all),
            jnp.stack(ks_list), jnp.stack(vs_list), jnp.stack(cv_list))
```

```python
import functools

import jax
import jax.numpy as jnp
from jax import lax
from jax.experimental import pallas as pl
from jax.experimental.pallas import tpu as pltpu

D_MODEL = 1024
N_HEADS = 16
HEAD_DIM = 64
PAGE_SIZE = 128
CHUNK = 128
CM_WIDTH = D_MODEL
CM_GROUPS = 8
CM_GROUP_DIM = CM_WIDTH // CM_GROUPS
D_FF = 2816
PLE_DIM = 256
DEPTH = 4
DEEPNORM_ALPHA = (2 * DEPTH) ** 0.25
LN_EPS = 1e-5
LOG2E = 1.4426950408889634

LANES = 128
MXU_TILE = 256
ROW_TILE = 512
ATTN_Q_ROWS = 1024
NORM_SLACK = 1.01
HEADS_PER_BLOCK = LANES // HEAD_DIM
N_HEAD_BLOCKS = N_HEADS // HEADS_PER_BLOCK
VMEM_LIMIT = 56 * 1024 * 1024
SAMPLE_TILE_PAGES = 4
EXIT_LOG2 = 160.0
EXP2_CAP = 120.0

F32 = jnp.float32
BF16 = jnp.bfloat16


def _const_spec(shape):
    nd = len(shape)
    return pl.BlockSpec(shape, lambda *_: (0,) * nd, pipeline_mode=pl.Buffered(1))


def _layer_spec(stacked, layer):
    nd = stacked.ndim
    return pl.BlockSpec((None,) + stacked.shape[1:], lambda *_: (layer,) + (0,) * (nd - 1),
                        pipeline_mode=pl.Buffered(1))


def _layer_norm(x, g, b):
    mu = jnp.mean(x, axis=-1, keepdims=True)
    xc = x - mu
    var = jnp.mean(xc * xc, axis=-1, keepdims=True)
    return xc * lax.rsqrt(var + LN_EPS) * g + b


def _softplus2(z2):
    return jnp.maximum(jnp.log2(1.0 + jnp.exp2(jnp.minimum(z2, EXP2_CAP))), z2)


def _split_bf16(x):
    hi = x.astype(BF16)
    lo = (x - hi.astype(F32)).astype(BF16)
    return hi, lo


def _neg_suffix_sum_weights(n):
    r = lax.broadcasted_iota(jnp.int32, (n, n), 0)
    c = lax.broadcasted_iota(jnp.int32, (n, n), 1)
    tri = jnp.where(r >= c, -1.0, 0.0).astype(BF16)
    return jnp.concatenate([tri, tri], axis=0)


Q_SCALE = HEAD_DIM ** -0.5 * LOG2E


def _qkv_prompt_kernel(x_ref, w_ref, *rest, tk, first):
    q_ref, vb_ref, kt_ref, vt_ref, ktb_ref = rest if first else rest[2:]
    xb = x_ref[...].astype(BF16)
    qkv = jnp.dot(xb, w_ref[...], preferred_element_type=F32)
    k = qkv[:, D_MODEL:2 * D_MODEL]
    v = qkv[:, 2 * D_MODEL:]
    q_ref[...] = (qkv[:, :D_MODEL] * Q_SCALE).astype(BF16)
    vb_ref[...] = v.astype(BF16)
    kt, vt = k.T, v.T
    if first:
        for l in range(kt_ref.shape[0]):
            kt_ref[l] = kt
            vt_ref[l] = vt
    else:
        kt_ref[...] = kt
        vt_ref[...] = vt
    for j in range(ktb_ref.shape[0]):
        ktb_ref[j] = kt[:, j * tk:(j + 1) * tk].astype(BF16)


def _qkv_prompt(x, w_bf, layer, kt_all, vt_all, *, tm, tk):
    b, s, d = x.shape
    n_layers = w_bf.shape[0]
    first = kt_all is None
    assert s % tm == 0 and tm % tk == 0 and first == (layer == 0)
    row_spec = pl.BlockSpec((None, tm, d), lambda bi, i: (bi, i, 0))
    if first:
        stack_spec = pl.BlockSpec((n_layers, None, d, tm), lambda bi, i: (0, bi, 0, i))
        stacks, stack_in_specs, aliases = (), [], {}
    else:
        stack_spec = pl.BlockSpec((None, None, d, tm), lambda bi, i: (layer, bi, 0, i))
        stacks, stack_in_specs = (kt_all, vt_all), [pl.BlockSpec(memory_space=pl.ANY)] * 2
        aliases = {2: 2, 3: 3}
    stack_shape = jax.ShapeDtypeStruct((n_layers, b, d, s), F32)
    return pl.pallas_call(
        functools.partial(_qkv_prompt_kernel, tk=tk, first=first),
        grid=(b, s // tm),
        in_specs=[row_spec, _layer_spec(w_bf, layer)] + stack_in_specs,
        out_specs=[row_spec, row_spec, stack_spec, stack_spec,
                   pl.BlockSpec((None, tm // tk, d, tk), lambda bi, i: (bi, i, 0, 0))],
        out_shape=[jax.ShapeDtypeStruct((b, s, d), BF16), jax.ShapeDtypeStruct((b, s, d), BF16),
                   stack_shape, stack_shape, jax.ShapeDtypeStruct((b, s // tk, d, tk), BF16)],
        input_output_aliases=aliases,
        compiler_params=pltpu.CompilerParams(
            dimension_semantics=("parallel", "parallel"), vmem_limit_bytes=VMEM_LIMIT),
        name="qkv_prompt",
    )(x, w_bf, *stacks)


def _qkv_sample_kernel(x_ref, w_ref, qrows_ref, k_ref, v_ref, ktb_ref, vb_ref, *, t_new):
    n_s = x_ref.shape[0]
    n_seq = n_s // t_new
    xb = x_ref[...].astype(BF16)
    qkv = jnp.dot(xb, w_ref[...], preferred_element_type=F32)
    k = qkv[:, D_MODEL:2 * D_MODEL]
    v = qkv[:, 2 * D_MODEL:]
    k_ref[...] = k
    v_ref[...] = v
    ktb_ref[...] = k.T.astype(BF16)
    vb_ref[...] = v.astype(BF16)
    q4 = (qkv[:, :D_MODEL] * Q_SCALE).reshape(n_seq, 1, t_new, D_MODEL)
    rows = jnp.broadcast_to(q4, (n_seq, N_HEADS, t_new, D_MODEL)).reshape(n_seq, LANES, D_MODEL)
    r = lax.broadcasted_iota(jnp.int32, (n_seq, LANES, D_MODEL), 1) // t_new
    c = lax.broadcasted_iota(jnp.int32, (n_seq, LANES, D_MODEL), 2) // HEAD_DIM
    qrows_ref[...] = jnp.where(r == c, rows, 0.0).astype(BF16)


def _qkv_sample(x, w_bf, layer, *, t_new):
    n_s, d = x.shape
    assert N_HEADS * t_new == LANES
    whole = lambda shape: pl.BlockSpec(shape, lambda i: (0,) * len(shape))
    out_shape = [jax.ShapeDtypeStruct((n_s // t_new, LANES, d), BF16),
                 jax.ShapeDtypeStruct((n_s, d), F32), jax.ShapeDtypeStruct((n_s, d), F32),
                 jax.ShapeDtypeStruct((d, n_s), BF16), jax.ShapeDtypeStruct((n_s, d), BF16)]
    return pl.pallas_call(
        functools.partial(_qkv_sample_kernel, t_new=t_new),
        grid=(1,),
        in_specs=[whole((n_s, d)), _layer_spec(w_bf, layer)],
        out_specs=[whole(o.shape) for o in out_shape],
        out_shape=out_shape,
        compiler_params=pltpu.CompilerParams(
            dimension_semantics=("arbitrary",), vmem_limit_bytes=VMEM_LIMIT),
        name="qkv_sample",
    )(x, w_bf)


def _sb_prompt_kernel(bias_ref, q_ref, kt_ref, v_ref, o_ref, knorm_ref, *, tq, tk):
    hb = pl.program_id(1)
    qi = pl.program_id(2)
    heads = range(HEADS_PER_BLOCK)
    n_band = tq // tk
    tiles_per_trip = n_band
    q2 = q_ref[...]
    qlane_i = lax.broadcasted_iota(jnp.int32, q2.shape, 1)
    qlane = qlane_i // HEAD_DIM
    vlane = lax.broadcasted_iota(jnp.int32, (tk, LANES), 1) // HEAD_DIM
    krow_i = lax.broadcasted_iota(jnp.int32, (LANES, tk), 0)
    biases = [bias_ref[hb * HEADS_PER_BLOCK + h] for h in heads]
    w_cum = _neg_suffix_sum_weights(tk)[:tk]
    q_heads, k_own_rows, k_bias_rows = [], [], []
    for h in heads:
        spare = (HEADS_PER_BLOCK - 1 - h) * HEAD_DIM
        ones = ((qlane_i == spare) | (qlane_i == spare + 1)).astype(BF16)
        q_heads.append(jnp.where(qlane == h, q2, ones))
        b_full = jnp.zeros((LANES, tk), F32) + biases[h]
        b_hi = b_full.astype(BF16)
        b_lo = (b_full - b_hi.astype(F32)).astype(BF16)
        k_own_rows.append(jnp.where(krow_i // HEAD_DIM == h, 1.0, 0.0).astype(BF16))
        k_bias_rows.append(jnp.where(krow_i == spare, b_full,
                                     jnp.where(krow_i == spare + 1, b_full - b_hi.astype(F32), 0.0)
                                     ).astype(BF16))

    def tile(kb, carry, r0):
        carries, acc = carry
        masked = r0 is not None
        r0 = r0 or 0
        rows = tq - r0
        keep_head_rows = lambda old, new: jnp.concatenate([old[:r0], new], axis=0) if r0 else new
        kt2 = kt_ref[kb]
        v2 = v_ref[pl.ds(pl.multiple_of(kb * tk, tk), tk), :]
        if masked:
            causal = (lax.broadcasted_iota(jnp.int32, (rows, tk), 1)
                      < lax.broadcasted_iota(jnp.int32, (rows, tk), 0))
        new_carries, a_parts = [], []
        for h in heads:
            kt_h = kt2 * k_own_rows[h] + k_bias_rows[h]
            z = jnp.dot(q_heads[h][r0:], kt_h, preferred_element_type=F32)
            sp = _softplus2(z)
            if masked:
                sp = jnp.where(causal, sp, 0.0)
            c = jnp.dot(sp.astype(BF16), w_cum,
                        preferred_element_type=F32) + carries[h][r0:]
            a = jnp.exp2(z + c)
            if masked:
                a = jnp.where(causal, a, 0.0)
            a_parts.append(a.astype(BF16))
            new_carries.append(keep_head_rows(carries[h], c[:, 0:1]))
        v_st = jnp.concatenate([jnp.where(vlane == h, v2, jnp.zeros_like(v2)) for h in heads], axis=0)
        upd = acc[r0:] + jnp.dot(jnp.concatenate(a_parts, axis=1), v_st, preferred_element_type=F32)
        return tuple(new_carries), keep_head_rows(acc, upd)

    carry = (tuple(jnp.zeros((tq, 1), F32) for _ in heads), jnp.zeros((tq, LANES), F32))
    first_kb = qi * n_band
    for j in reversed(range(n_band)):
        carry = tile(first_kb + j, carry, j * tk)

    @pl.when(qi == 0)
    def _():
        kt_all = kt_ref[...].astype(F32)
        k_sq = kt_all * kt_all
        for h in heads:
            per_key = jnp.sum(k_sq[:, h * HEAD_DIM:(h + 1) * HEAD_DIM, :], axis=1)
            knorm_ref[h] = jnp.max(per_key)

    q_sq = q2.astype(F32) * q2.astype(F32)
    z_max = []
    for h in heads:
        q_norm2 = jnp.sum(jnp.where(qlane == h, q_sq, 0.0), axis=1, keepdims=True)
        z_max.append(jnp.sqrt(q_norm2 * knorm_ref[h]) * NORM_SLACK + jnp.abs(jnp.zeros((tq, 1), F32)
                                                                     + biases[h]))

    def log2_weight_bound(carries):
        return jnp.max(jnp.maximum(*(carries[h] + z_max[h] for h in heads)))

    def more_to_do(state):
        t, bound, _ = state
        return (t < first_kb // tiles_per_trip) & (bound > -EXIT_LOG2)

    def past_tiles(state):
        t, _, cr = state
        for j in range(tiles_per_trip):
            cr = tile(first_kb - 1 - (t * tiles_per_trip + j), cr, None)
        return t + 1, log2_weight_bound(cr[0]), cr

    _, _, carry = lax.while_loop(more_to_do, past_tiles,
                                 (jnp.int32(0), log2_weight_bound(carry[0]), carry))
    o_ref[...] = carry[1].astype(o_ref.dtype)


def _sb_prompt(q_bf, kt_bf, v_bf, bias, *, tq):
    b, s, d = q_bf.shape
    tk = kt_bf.shape[-1]
    assert tq % tk == 0 and s % tq == 0
    grid = (b, N_HEAD_BLOCKS, s // tq)
    return pl.pallas_call(
        functools.partial(_sb_prompt_kernel, tq=tq, tk=tk),
        grid=grid,
        in_specs=[
            pl.BlockSpec(memory_space=pltpu.SMEM),
            pl.BlockSpec((None, tq, LANES), lambda bi, hb, i: (bi, i, hb)),
            pl.BlockSpec((None, s // tk, LANES, tk), lambda bi, hb, i: (bi, 0, hb, 0)),
            pl.BlockSpec((None, s, LANES), lambda bi, hb, i: (bi, 0, hb)),
        ],
        out_specs=pl.BlockSpec((None, tq, LANES), lambda bi, hb, i: (bi, i, hb)),
        out_shape=jax.ShapeDtypeStruct((b, s, d), BF16),
        scratch_shapes=[pltpu.SMEM((HEADS_PER_BLOCK,), F32)],
        compiler_params=pltpu.CompilerParams(
            dimension_semantics=("parallel", "parallel", "arbitrary"),
            vmem_limit_bytes=VMEM_LIMIT),
        name="sb_prompt",
    )(bias, q_bf, kt_bf, v_bf)


def _sb_sample_kernel(pt_ref, bias_ref, q_ref, kn_ref, vn_ref, *rest, n_pages_step, t_new,
                      step=None):
    k_refs = rest[:n_pages_step]
    v_refs = rest[n_pages_step:2 * n_pages_step]
    o_ref, carry_ref, acc_ref = rest[2 * n_pages_step:]
    seq, g, n_g = step or (pl.program_id(0), pl.program_id(1), pl.num_programs(1))
    q_rows = q_ref[...]
    bias = bias_ref[...]

    def lane_tiled(x, n):
        return x[:, :n] if n <= LANES else jnp.concatenate([x] * (n // LANES), axis=1)

    def tile(s, mask, weigh_values):
        n = s.shape[1]
        z = s + lane_tiled(bias, n)
        sp = _softplus2(z)
        if mask is not None:
            sp = jnp.where(mask, sp, 0.0)
        hi, lo = _split_bf16(sp)
        c = jnp.dot(jnp.concatenate([hi, lo], axis=1), _neg_suffix_sum_weights(n),
                    preferred_element_type=F32)
        c = c + lane_tiled(carry_ref[...], n)
        a = jnp.exp2(z + c)
        if mask is not None:
            a = jnp.where(mask, a, 0.0)
        acc_ref[...] += weigh_values(a.astype(BF16))
        carry_ref[...] = jnp.broadcast_to(c[:, 0:1], carry_ref.shape)

    @pl.when(g == 0)
    def _():
        carry_ref[...] = jnp.zeros_like(carry_ref)
        acc_ref[...] = jnp.zeros_like(acc_ref)
        kn, vn = kn_ref[...], vn_ref[...]
        n = kn.shape[1]
        qry_i = lax.broadcasted_iota(jnp.int32, (LANES, n), 0) % t_new
        key_i = lax.broadcasted_iota(jnp.int32, (LANES, n), 1)
        mask = (key_i // t_new == seq) & (key_i % t_new < qry_i)
        tile(jnp.dot(q_rows, kn, preferred_element_type=F32), mask,
             lambda a: jnp.dot(a, vn, preferred_element_type=F32))

    for lo_page in reversed(range(0, n_pages_step, SAMPLE_TILE_PAGES)):
        sel = slice(lo_page, lo_page + SAMPLE_TILE_PAGES)
        kt = jnp.concatenate([r[...].astype(BF16) for r in k_refs[sel]], axis=1)
        vt = jnp.concatenate([r[...].astype(BF16) for r in v_refs[sel]], axis=1)
        tile(jnp.dot(q_rows, kt, preferred_element_type=F32), None,
             lambda a: lax.dot_general(a, vt, (((1,), (1,)), ((), ())),
                                       preferred_element_type=F32))

    @pl.when(g == n_g - 1)
    def _():
        acc = acc_ref[...]
        r = lax.broadcasted_iota(jnp.int32, acc.shape, 0) // t_new
        c = lax.broadcasted_iota(jnp.int32, acc.shape, 1) // HEAD_DIM
        own = jnp.where(r == c, acc, 0.0).reshape(N_HEADS, t_new, D_MODEL)
        o_ref[...] = jnp.sum(own, axis=0).astype(o_ref.dtype)


def _sb_sample(q_rows, kt_new, v_new, cache_k, cache_v, layer, page_table, bias, *, t_new,
               pages_per_step):
    n, _, d = q_rows.shape
    n_pages = page_table.shape[1]
    assert n_pages % pages_per_step == 0
    n_steps = n_pages // pages_per_step
    n_layers, n_pool = cache_k.shape[:2]
    ck = cache_k.transpose(0, 1, 3, 4, 2).reshape(n_layers, n_pool, d, PAGE_SIZE)
    cv = cache_v.transpose(0, 1, 3, 4, 2).reshape(n_layers, n_pool, d, PAGE_SIZE)
    bias_rows = jnp.broadcast_to(jnp.repeat(bias.astype(F32), t_new)[:, None], (LANES, LANES))

    def page_spec(i):
        return pl.BlockSpec(
            (None, None, d, PAGE_SIZE),
            lambda s, g, pt: (layer, pt[s, (n_steps - 1 - g) * pages_per_step + i], 0, 0))

    whole = lambda a: pl.BlockSpec(a.shape, lambda s, g, pt: (0,) * a.ndim)
    grid_spec = pltpu.PrefetchScalarGridSpec(
        num_scalar_prefetch=1,
        grid=(n, n_steps),
        in_specs=[whole(bias_rows), pl.BlockSpec((None, LANES, d), lambda s, g, pt: (s, 0, 0)),
                  whole(kt_new), whole(v_new)]
                 + [page_spec(i) for i in range(pages_per_step)] * 2,
        out_specs=pl.BlockSpec((None, t_new, d), lambda s, g, pt: (s, 0, 0)),
        scratch_shapes=[pltpu.VMEM((LANES, LANES), F32), pltpu.VMEM((LANES, d), F32)],
    )
    return pl.pallas_call(
        functools.partial(_sb_sample_kernel, n_pages_step=pages_per_step, t_new=t_new),
        grid_spec=grid_spec,
        out_shape=jax.ShapeDtypeStruct((n, t_new, d), BF16),
        compiler_params=pltpu.CompilerParams(
            dimension_semantics=("parallel", "arbitrary"), vmem_limit_bytes=VMEM_LIMIT),
        name="sb_sample",
    )(page_table, bias_rows, q_rows, kt_new, v_new,
      *([ck] * pages_per_step), *([cv] * pages_per_step))


def _sb_fused_kernel(pt_ref, pbias_ref, q_ref, kt_ref, v_ref, sbias_ref, qrows_ref, kn_ref, vn_ref,
                     *rest, tq, tk, n_pages_step, t_new, steps_per_seq):
    pages = rest[:2 * n_pages_step]
    o_ref, os_ref, knorm_ref, carry_ref, acc_ref = rest[2 * n_pages_step:]
    _sb_prompt_kernel(pbias_ref, q_ref, kt_ref, v_ref, o_ref, knorm_ref, tq=tq, tk=tk)
    lin = ((pl.program_id(0) * pl.num_programs(1) + pl.program_id(1)) * pl.num_programs(2)
           + pl.program_id(2))
    _sb_sample_kernel(pt_ref, sbias_ref, qrows_ref, kn_ref, vn_ref, *pages, os_ref, carry_ref,
                      acc_ref, n_pages_step=n_pages_step, t_new=t_new,
                      step=(lin // steps_per_seq, lin % steps_per_seq, steps_per_seq))


def _fused_steps_per_seq(q_bf, q_rows, page_table, tq):
    b, s, _ = q_bf.shape
    total = b * N_HEAD_BLOCKS * (s // tq)
    n_seq, n_pages = q_rows.shape[0], page_table.shape[1]
    if total % n_seq or n_pages % (total // n_seq):
        return None
    return total // n_seq


def _sb_fused(q_bf, kt_bf, v_bf, bias, q_rows, kt_new, v_new, cache_k, cache_v, layer, page_table,
              *, tq, t_new, steps_per_seq):
    b, s, d = q_bf.shape
    tk = kt_bf.shape[-1]
    n_q = s // tq
    n_seq, n_pages = q_rows.shape[0], page_table.shape[1]
    pages_per_step = n_pages // steps_per_seq
    assert tq % tk == 0 and s % tq == 0
    n_layers, n_pool = cache_k.shape[:2]
    ck = cache_k.transpose(0, 1, 3, 4, 2).reshape(n_layers, n_pool, d, PAGE_SIZE)
    cv = cache_v.transpose(0, 1, 3, 4, 2).reshape(n_layers, n_pool, d, PAGE_SIZE)
    bias_rows = jnp.broadcast_to(jnp.repeat(bias.astype(F32), t_new)[:, None], (LANES, LANES))

    def seq_step(bi, hb, i):
        lin = (bi * N_HEAD_BLOCKS + hb) * n_q + i
        return lin // steps_per_seq, lin % steps_per_seq

    def page_spec(k):
        def index(bi, hb, i, pt):
            seq, g = seq_step(bi, hb, i)
            return layer, pt[seq, (steps_per_seq - 1 - g) * pages_per_step + k], 0, 0
        return pl.BlockSpec((None, None, d, PAGE_SIZE), index)

    whole = lambda a: pl.BlockSpec(a.shape, lambda *_: (0,) * a.ndim)
    per_seq = lambda rows: pl.BlockSpec((None, rows, d),
                                        lambda bi, hb, i, pt: (seq_step(bi, hb, i)[0], 0, 0))
    grid_spec = pltpu.PrefetchScalarGridSpec(
        num_scalar_prefetch=1,
        grid=(b, N_HEAD_BLOCKS, n_q),
        in_specs=[pl.BlockSpec(memory_space=pltpu.SMEM),
                  pl.BlockSpec((None, tq, LANES), lambda bi, hb, i, pt: (bi, i, hb)),
                  pl.BlockSpec((None, s // tk, LANES, tk), lambda bi, hb, i, pt: (bi, 0, hb, 0)),
                  pl.BlockSpec((None, s, LANES), lambda bi, hb, i, pt: (bi, 0, hb)),
                  whole(bias_rows), per_seq(LANES), whole(kt_new), whole(v_new)]
                 + [page_spec(k) for k in range(pages_per_step)] * 2,
        out_specs=[pl.BlockSpec((None, tq, LANES), lambda bi, hb, i, pt: (bi, i, hb)),
                   per_seq(t_new)],
        scratch_shapes=[pltpu.SMEM((HEADS_PER_BLOCK,), F32),
                        pltpu.VMEM((LANES, LANES), F32), pltpu.VMEM((LANES, d), F32)],
    )
    return pl.pallas_call(
        functools.partial(_sb_fused_kernel, tq=tq, tk=tk, n_pages_step=pages_per_step,
                          t_new=t_new, steps_per_seq=steps_per_seq),
        grid_spec=grid_spec,
        out_shape=[jax.ShapeDtypeStruct((b, s, d), BF16),
                   jax.ShapeDtypeStruct((n_seq, t_new, d), BF16)],
        compiler_params=pltpu.CompilerParams(
            dimension_semantics=("arbitrary", "arbitrary", "arbitrary"),
            vmem_limit_bytes=VMEM_LIMIT),
        name="sb_fused",
    )(page_table, bias, q_bf, kt_bf, v_bf, bias_rows, q_rows, kt_new, v_new,
      *([ck] * pages_per_step), *([cv] * pages_per_step))


def _cm_kernel(x_ref, wuv_ref, g_ref, b_ref, ws_ref, bias_ref, gate_ref, *rest, chunk, with_v):
    rows = ws_ref.shape[-1]
    xb = x_ref[...].astype(BF16)
    z = jnp.dot(xb, wuv_ref[...], preferred_element_type=F32)
    z = 0.5 * z * (1.0 + lax.erf(z * (0.5 ** 0.5)))
    u = z[:, :CM_WIDTH]
    v = _layer_norm(z[:, CM_WIDTH:], g_ref[...], b_ref[...])
    if with_v:
        rest[0][...] = v
    r = lax.broadcasted_iota(jnp.int32, (rows, rows), 0)
    c = lax.broadcasted_iota(jnp.int32, (rows, rows), 1)
    keep = c <= r
    if chunk < rows:
        keep &= (r // chunk) == (c // chunk)
    w_sp = [jnp.where(keep, ws_ref[g], 0.0).astype(BF16) for g in range(CM_GROUPS)]
    vb = v.astype(BF16)
    bias = bias_ref[...]
    for ci in range(x_ref.shape[0] // rows):
        rs = slice(ci * rows, (ci + 1) * rows)
        for g in range(CM_GROUPS):
            cs = slice(g * CM_GROUP_DIM, (g + 1) * CM_GROUP_DIM)
            mixed = jnp.dot(w_sp[g], vb[rs, cs], preferred_element_type=F32) + bias[:, cs]
            gate_ref[rs, cs] = (u[rs, cs] * mixed).astype(gate_ref.dtype)


def _chunk_mlp(x, w_uv_bf, ln_g, ln_b, layer, w_sp, bias_full, *, tm, chunk, with_v):
    n, d = x.shape
    rows = w_sp.shape[-1]
    assert n % tm == 0 and tm % rows == 0 and rows % chunk == 0
    row_spec = pl.BlockSpec((tm, d), lambda i: (i, 0))
    out_shape = [jax.ShapeDtypeStruct((n, CM_WIDTH), BF16)]
    out_specs = [pl.BlockSpec((tm, CM_WIDTH), lambda i: (i, 0))]
    if with_v:
        out_shape.append(jax.ShapeDtypeStruct((n, CM_WIDTH), F32))
        out_specs.append(pl.BlockSpec((tm, CM_WIDTH), lambda i: (i, 0)))
    return pl.pallas_call(
        functools.partial(_cm_kernel, chunk=chunk, with_v=with_v),
        grid=(n // tm,),
        in_specs=[row_spec, _layer_spec(w_uv_bf, layer), _layer_spec(ln_g, layer),
                  _layer_spec(ln_b, layer), _const_spec(w_sp.shape), _const_spec(bias_full.shape)],
        out_specs=out_specs,
        out_shape=out_shape,
        compiler_params=pltpu.CompilerParams(
            dimension_semantics=("parallel",), vmem_limit_bytes=VMEM_LIMIT),
        name="chunk_mlp",
    )(x, w_uv_bf, ln_g, ln_b, w_sp, bias_full)


def _tail_kernel(x_ref, pre_ref, p_ref, wo_ref, g1_ref, b1_ref, win_ref, wout_ref,
                 g2_ref, b2_ref, wg_ref, bg_ref, wp_ref, o_ref):
    mix = jnp.dot(pre_ref[...], wo_ref[...], preferred_element_type=F32)
    h = _layer_norm(DEEPNORM_ALPHA * x_ref[...] + mix, g1_ref[...], b1_ref[...])
    gu = jnp.dot(h.astype(BF16), win_ref[...], preferred_element_type=F32)
    gate_half = gu[:, :D_FF]
    act = gate_half * jax.nn.sigmoid(gate_half) * gu[:, D_FF:]
    f = jnp.dot(act.astype(BF16), wout_ref[...], preferred_element_type=F32)
    h = _layer_norm(DEEPNORM_ALPHA * h + f, g2_ref[...], b2_ref[...])
    ple_gate = jax.nn.sigmoid(
        jnp.dot(h.astype(BF16), wg_ref[...], preferred_element_type=F32) + bg_ref[...])
    ple = jnp.dot(p_ref[...].astype(BF16), wp_ref[...], preferred_element_type=F32)
    o_ref[...] = h + ple_gate * ple


def _layer_tail(x, pre_bf, p, layer, wo, mixer_layer, params, *, tm):
    n, d = x.shape
    assert n % tm == 0
    row = lambda w: pl.BlockSpec((tm, w), lambda i: (i, 0))
    return pl.pallas_call(
        _tail_kernel,
        grid=(n // tm,),
        in_specs=[row(d), row(d), pl.BlockSpec((None, tm, PLE_DIM), lambda i: (layer, i, 0)),
                  _layer_spec(wo, mixer_layer)] + [_layer_spec(a, layer) for a in params],
        out_specs=row(d),
        out_shape=jax.ShapeDtypeStruct((n, d), F32),
        compiler_params=pltpu.CompilerParams(
            dimension_semantics=("parallel",), vmem_limit_bytes=VMEM_LIMIT),
        name="layer_tail",
    )(x, pre_bf, p, wo, *params)


def _tile_sizes(n_rows, preferred):
    t = min(preferred, n_rows)
    while n_rows % t:
        t -= 8
    return t


def kernel(x_prompt, x_sample, cache_k, cache_v, page_table, p_prompt, p_sample, w_qkv, w_o_sb,
           sb_bias, w_uv, ln_v_g, ln_v_b, w_s, b_s, w_o_cm, ln1_g, ln1_b, ln2_g, ln2_b,
           w_ffn_in, w_ffn_out, w_ple_gate, b_ple_gate, w_ple_proj):
    b, s, d = x_prompt.shape
    n_seq, t_new, _ = x_sample.shape
    n_p, n_s = b * s, n_seq * t_new
    depth = w_ffn_in.shape[0]
    assert depth == DEPTH and s % CHUNK == 0
    attn_tile = MXU_TILE
    attn_tq = _tile_sizes(s, ATTN_Q_ROWS)
    tm_p = _tile_sizes(s, ROW_TILE)
    pages_per_step = next(p for p in (16, 8, 4, 2, 1) if page_table.shape[1] % p == 0)

    vec = lambda a: a.reshape(a.shape[0], 1, a.shape[1])
    w_qkv_bf, w_uv_bf = w_qkv.astype(BF16), w_uv.astype(BF16)
    w_o_sb_bf, w_o_cm_bf = w_o_sb.astype(BF16), w_o_cm.astype(BF16)
    ln_v_g3, ln_v_b3 = vec(ln_v_g), vec(ln_v_b)
    tail_params = (vec(ln1_g), vec(ln1_b), w_ffn_in.astype(BF16), w_ffn_out.astype(BF16),
                   vec(ln2_g), vec(ln2_b), w_ple_gate.astype(BF16), vec(b_ple_gate),
                   w_ple_proj.astype(BF16))
    p_prompt2 = p_prompt.reshape(depth, n_p, PLE_DIM)
    p_sample2 = p_sample.reshape(depth, n_s, PLE_DIM)

    xp = x_prompt
    xs = x_sample.reshape(n_s, d)
    kt_all = vt_all = None
    ks_list, vs_list, cv_list = [], [], []
    for i in range(depth):
        j = i // 2
        if i % 2 == 0:
            q_bf, v_bf, kt_all, vt_all, kt_bf = _qkv_prompt(xp, w_qkv_bf, j, kt_all, vt_all,
                                                             tm=tm_p, tk=attn_tile)
            bias2 = sb_bias[j] * LOG2E
            q_rows, kn, vn, knt_bf, vn_bf = _qkv_sample(xs, w_qkv_bf, j, t_new=t_new)
            steps_per_seq = _fused_steps_per_seq(q_bf, q_rows, page_table, attn_tq)
            if steps_per_seq:
                pre_p, pre_s = _sb_fused(q_bf, kt_bf, v_bf, bias2, q_rows, knt_bf, vn_bf, cache_k,
                                         cache_v, j, page_table, tq=attn_tq, t_new=t_new,
                                         steps_per_seq=steps_per_seq)
            else:
                pre_p = _sb_prompt(q_bf, kt_bf, v_bf, bias2, tq=attn_tq)
                pre_s = _sb_sample(q_rows, knt_bf, vn_bf, cache_k, cache_v, j, page_table, bias2,
                                   t_new=t_new, pages_per_step=pages_per_step)
            pre_p, pre_s = pre_p.reshape(n_p, d), pre_s.reshape(n_s, d)
            ks_list.append(kn.reshape(n_seq, t_new, N_HEADS, HEAD_DIM))
            vs_list.append(vn.reshape(n_seq, t_new, N_HEADS, HEAD_DIM))
            w_o = w_o_sb_bf
        else:
            bias_p = jnp.repeat(b_s[j].T, CM_GROUP_DIM, axis=1)
            pre_p, = _chunk_mlp(xp.reshape(n_p, d), w_uv_bf, ln_v_g3, ln_v_b3, j, w_s[j], bias_p,
                                tm=tm_p, chunk=CHUNK, with_v=False)
            w_rep = jnp.tile(w_s[j][:, :t_new, :t_new], (1, n_seq, n_seq))
            bias_s = jnp.tile(jnp.repeat(b_s[j].T[:t_new], CM_GROUP_DIM, axis=1), (n_seq, 1))
            pre_s, v_s = _chunk_mlp(xs, w_uv_bf, ln_v_g3, ln_v_b3, j, w_rep, bias_s,
                                    tm=n_s, chunk=t_new, with_v=True)
            cv_list.append(v_s.reshape(n_seq, t_new, CM_WIDTH))
            w_o = w_o_cm_bf

        xp = _layer_tail(xp.reshape(n_p, d), pre_p, p_prompt2, i, w_o, j, tail_params,
                         tm=_tile_sizes(n_p, ROW_TILE)).reshape(b, s, d)
        xs = _layer_tail(xs, pre_s, p_sample2, i, w_o, j, tail_params,
                         tm=_tile_sizes(n_s, ROW_TILE))
    heads_last = lambda a: a.reshape(-1, b, N_HEADS, HEAD_DIM, s).transpose(0, 1, 4, 2, 3)
    return (xp, xs.reshape(n_seq, t_new, d), heads_last(kt_all), heads_last(vt_all),
            jnp.stack(ks_list), jnp.stack(vs_list), jnp.stack(cv_list))
```

```python
import functools

import jax
import jax.numpy as jnp
from jax import lax
from jax.experimental import pallas as pl
from jax.experimental.pallas import tpu as pltpu

D_MODEL = 1024
N_HEADS = 16
HEAD_DIM = 64
PAGE_SIZE = 128
CHUNK = 128
CM_WIDTH = D_MODEL
CM_GROUPS = 8
CM_GROUP_DIM = CM_WIDTH // CM_GROUPS
D_FF = 2816
PLE_DIM = 256
DEPTH = 4
DEEPNORM_ALPHA = (2 * DEPTH) ** 0.25
LN_EPS = 1e-5
LOG2E = 1.4426950408889634

LANES = 128
MXU_TILE = 256
ROW_TILE = 512
ATTN_Q_ROWS = 1024
NORM_SLACK = 1.01
HEADS_PER_BLOCK = LANES // HEAD_DIM
N_HEAD_BLOCKS = N_HEADS // HEADS_PER_BLOCK
VMEM_LIMIT = 56 * 1024 * 1024
SAMPLE_TILE_PAGES = 4
EXIT_LOG2 = 160.0
EXP2_CAP = 120.0

F32 = jnp.float32
BF16 = jnp.bfloat16


def _const_spec(shape):
    nd = len(shape)
    return pl.BlockSpec(shape, lambda *_: (0,) * nd, pipeline_mode=pl.Buffered(1))


def _layer_spec(stacked, layer):
    nd = stacked.ndim
    return pl.BlockSpec((None,) + stacked.shape[1:], lambda *_: (layer,) + (0,) * (nd - 1),
                        pipeline_mode=pl.Buffered(1))


def _layer_norm(x, g, b):
    mu = jnp.mean(x, axis=-1, keepdims=True)
    xc = x - mu
    var = jnp.mean(xc * xc, axis=-1, keepdims=True)
    return xc * lax.rsqrt(var + LN_EPS) * g + b


def _softplus2(z2):
    return jnp.maximum(jnp.log2(1.0 + jnp.exp2(jnp.minimum(z2, EXP2_CAP))), z2)


def _split_bf16(x):
    hi = x.astype(BF16)
    lo = (x - hi.astype(F32)).astype(BF16)
    return hi, lo


def _neg_suffix_sum_weights(n):
    r = lax.broadcasted_iota(jnp.int32, (n, n), 0)
    c = lax.broadcasted_iota(jnp.int32, (n, n), 1)
    tri = jnp.where(r >= c, -1.0, 0.0).astype(BF16)
    return jnp.concatenate([tri, tri], axis=0)


Q_SCALE = HEAD_DIM ** -0.5 * LOG2E


def _qkv_prompt_kernel(x_ref, w_ref, *rest, tk, first):
    q_ref, vb_ref, kt_ref, vt_ref, ktb_ref = rest if first else rest[2:]
    xb = x_ref[...].astype(BF16)
    qkv = jnp.dot(xb, w_ref[...], preferred_element_type=F32)
    k = qkv[:, D_MODEL:2 * D_MODEL]
    v = qkv[:, 2 * D_MODEL:]
    q_ref[...] = (qkv[:, :D_MODEL] * Q_SCALE).astype(BF16)
    vb_ref[...] = v.astype(BF16)
    kt, vt = k.T, v.T
    if first:
        for l in range(kt_ref.shape[0]):
            kt_ref[l] = kt
            vt_ref[l] = vt
    else:
        kt_ref[...] = kt
        vt_ref[...] = vt
    for j in range(ktb_ref.shape[0]):
        ktb_ref[j] = kt[:, j * tk:(j + 1) * tk].astype(BF16)


def _qkv_prompt(x, w_bf, layer, kt_all, vt_all, *, tm, tk):
    b, s, d = x.shape
    n_layers = w_bf.shape[0]
    first = kt_all is None
    assert s % tm == 0 and tm % tk == 0 and first == (layer == 0)
    row_spec = pl.BlockSpec((None, tm, d), lambda bi, i: (bi, i, 0))
    if first:
        stack_spec = pl.BlockSpec((n_layers, None, d, tm), lambda bi, i: (0, bi, 0, i))
        stacks, stack_in_specs, aliases = (), [], {}
    else:
        stack_spec = pl.BlockSpec((None, None, d, tm), lambda bi, i: (layer, bi, 0, i))
        stacks, stack_in_specs = (kt_all, vt_all), [pl.BlockSpec(memory_space=pl.ANY)] * 2
        aliases = {2: 2, 3: 3}
    stack_shape = jax.ShapeDtypeStruct((n_layers, b, d, s), F32)
    return pl.pallas_call(
        functools.partial(_qkv_prompt_kernel, tk=tk, first=first),
        grid=(b, s // tm),
        in_specs=[row_spec, _layer_spec(w_bf, layer)] + stack_in_specs,
        out_specs=[row_spec, row_spec, stack_spec, stack_spec,
                   pl.BlockSpec((None, tm // tk, d, tk), lambda bi, i: (bi, i, 0, 0))],
        out_shape=[jax.ShapeDtypeStruct((b, s, d), BF16), jax.ShapeDtypeStruct((b, s, d), BF16),
                   stack_shape, stack_shape, jax.ShapeDtypeStruct((b, s // tk, d, tk), BF16)],
        input_output_aliases=aliases,
        compiler_params=pltpu.CompilerParams(
            dimension_semantics=("parallel", "parallel"), vmem_limit_bytes=VMEM_LIMIT),
        name="qkv_prompt",
    )(x, w_bf, *stacks)


def _qkv_sample_kernel(x_ref, w_ref, qrows_ref, k_ref, v_ref, ktb_ref, vb_ref, *, t_new):
    n_s = x_ref.shape[0]
    n_seq = n_s // t_new
    xb = x_ref[...].astype(BF16)
    qkv = jnp.dot(xb, w_ref[...], preferred_element_type=F32)
    k = qkv[:, D_MODEL:2 * D_MODEL]
    v = qkv[:, 2 * D_MODEL:]
    k_ref[...] = k
    v_ref[...] = v
    ktb_ref[...] = k.T.astype(BF16)
    vb_ref[...] = v.astype(BF16)
    q4 = (qkv[:, :D_MODEL] * Q_SCALE).reshape(n_seq, 1, t_new, D_MODEL)
    rows = jnp.broadcast_to(q4, (n_seq, N_HEADS, t_new, D_MODEL)).reshape(n_seq, LANES, D_MODEL)
    r = lax.broadcasted_iota(jnp.int32, (n_seq, LANES, D_MODEL), 1) // t_new
    c = lax.broadcasted_iota(jnp.int32, (n_seq, LANES, D_MODEL), 2) // HEAD_DIM
    qrows_ref[...] = jnp.where(r == c, rows, 0.0).astype(BF16)


def _qkv_sample(x, w_bf, layer, *, t_new):
    n_s, d = x.shape
    assert N_HEADS * t_new == LANES
    whole = lambda shape: pl.BlockSpec(shape, lambda i: (0,) * len(shape))
    out_shape = [jax.ShapeDtypeStruct((n_s // t_new, LANES, d), BF16),
                 jax.ShapeDtypeStruct((n_s, d), F32), jax.ShapeDtypeStruct((n_s, d), F32),
                 jax.ShapeDtypeStruct((d, n_s), BF16), jax.ShapeDtypeStruct((n_s, d), BF16)]
    return pl.pallas_call(
        functools.partial(_qkv_sample_kernel, t_new=t_new),
        grid=(1,),
        in_specs=[whole((n_s, d)), _layer_spec(w_bf, layer)],
        out_specs=[whole(o.shape) for o in out_shape],
        out_shape=out_shape,
        compiler_params=pltpu.CompilerParams(
            dimension_semantics=("arbitrary",), vmem_limit_bytes=VMEM_LIMIT),
        name="qkv_sample",
    )(x, w_bf)


def _sb_prompt_kernel(bias_ref, q_ref, kt_ref, v_ref, o_ref, knorm_ref, *, tq, tk,
                      beside_band=None):
    hb = pl.program_id(1)
    qi = pl.program_id(2)
    heads = range(HEADS_PER_BLOCK)

    @pl.when(qi == 0)
    def _():
        kt_all = kt_ref[...].astype(F32)
        k_sq = kt_all * kt_all
        for h in heads:
            per_key = jnp.sum(k_sq[:, h * HEAD_DIM:(h + 1) * HEAD_DIM, :], axis=1)
            knorm_ref[h] = jnp.max(per_key)

    n_band = tq // tk
    tiles_per_trip = n_band
    q2 = q_ref[...]
    qlane_i = lax.broadcasted_iota(jnp.int32, q2.shape, 1)
    qlane = qlane_i // HEAD_DIM
    vlane = lax.broadcasted_iota(jnp.int32, (tk, LANES), 1) // HEAD_DIM
    krow_i = lax.broadcasted_iota(jnp.int32, (LANES, tk), 0)
    biases = [bias_ref[hb * HEADS_PER_BLOCK + h] for h in heads]
    w_cum = _neg_suffix_sum_weights(tk)[:tk]
    q_heads, k_own_rows, k_bias_rows = [], [], []
    for h in heads:
        spare = (HEADS_PER_BLOCK - 1 - h) * HEAD_DIM
        ones = ((qlane_i == spare) | (qlane_i == spare + 1)).astype(BF16)
        q_heads.append(jnp.where(qlane == h, q2, ones))
        b_full = jnp.zeros((LANES, tk), F32) + biases[h]
        b_hi = b_full.astype(BF16)
        b_lo = (b_full - b_hi.astype(F32)).astype(BF16)
        k_own_rows.append(jnp.where(krow_i // HEAD_DIM == h, 1.0, 0.0).astype(BF16))
        k_bias_rows.append(jnp.where(krow_i == spare, b_full,
                                     jnp.where(krow_i == spare + 1, b_full - b_hi.astype(F32), 0.0)
                                     ).astype(BF16))

    def tile(kb, carry, r0):
        carries, acc = carry
        masked = r0 is not None
        r0 = r0 or 0
        rows = tq - r0
        keep_head_rows = lambda old, new: jnp.concatenate([old[:r0], new], axis=0) if r0 else new
        kt2 = kt_ref[kb]
        v2 = v_ref[pl.ds(pl.multiple_of(kb * tk, tk), tk), :]
        if masked:
            causal = (lax.broadcasted_iota(jnp.int32, (rows, tk), 1)
                      < lax.broadcasted_iota(jnp.int32, (rows, tk), 0))
        new_carries, a_parts = [], []
        for h in heads:
            kt_h = kt2 * k_own_rows[h] + k_bias_rows[h]
            z = jnp.dot(q_heads[h][r0:], kt_h, preferred_element_type=F32)
            sp = _softplus2(z)
            if masked:
                sp = jnp.where(causal, sp, 0.0)
            c = jnp.dot(sp.astype(BF16), w_cum,
                        preferred_element_type=F32) + carries[h][r0:]
            a = jnp.exp2(z + c)
            if masked:
                a = jnp.where(causal, a, 0.0)
            a_parts.append(a.astype(BF16))
            new_carries.append(keep_head_rows(carries[h], c[:, 0:1]))
        v_st = jnp.concatenate([jnp.where(vlane == h, v2, jnp.zeros_like(v2)) for h in heads], axis=0)
        upd = acc[r0:] + jnp.dot(jnp.concatenate(a_parts, axis=1), v_st, preferred_element_type=F32)
        return tuple(new_carries), keep_head_rows(acc, upd)

    carry = (tuple(jnp.zeros((tq, 1), F32) for _ in heads), jnp.zeros((tq, LANES), F32))
    first_kb = qi * n_band
    for j in reversed(range(n_band)):
        carry = tile(first_kb + j, carry, j * tk)
    if beside_band is not None:
        beside_band()

    q_sq = q2.astype(F32) * q2.astype(F32)
    z_max = []
    for h in heads:
        q_norm2 = jnp.sum(jnp.where(qlane == h, q_sq, 0.0), axis=1, keepdims=True)
        z_max.append(jnp.sqrt(q_norm2 * knorm_ref[h]) * NORM_SLACK + jnp.abs(jnp.zeros((tq, 1), F32)
                                                                     + biases[h]))

    def log2_weight_bound(carries):
        return jnp.max(jnp.maximum(*(carries[h] + z_max[h] for h in heads)))

    def more_to_do(state):
        t, bound, _ = state
        return (t < first_kb // tiles_per_trip) & (bound > -EXIT_LOG2)

    def past_tiles(state):
        t, _, cr = state
        for j in range(tiles_per_trip):
            cr = tile(first_kb - 1 - (t * tiles_per_trip + j), cr, None)
        return t + 1, log2_weight_bound(cr[0]), cr

    _, _, carry = lax.while_loop(more_to_do, past_tiles,
                                 (jnp.int32(0), log2_weight_bound(carry[0]), carry))
    o_ref[...] = carry[1].astype(o_ref.dtype)


def _sb_prompt(q_bf, kt_bf, v_bf, bias, *, tq):
    b, s, d = q_bf.shape
    tk = kt_bf.shape[-1]
    assert tq % tk == 0 and s % tq == 0
    grid = (b, N_HEAD_BLOCKS, s // tq)
    return pl.pallas_call(
        functools.partial(_sb_prompt_kernel, tq=tq, tk=tk),
        grid=grid,
        in_specs=[
            pl.BlockSpec(memory_space=pltpu.SMEM),
            pl.BlockSpec((None, tq, LANES), lambda bi, hb, i: (bi, i, hb)),
            pl.BlockSpec((None, s // tk, LANES, tk), lambda bi, hb, i: (bi, 0, hb, 0)),
            pl.BlockSpec((None, s, LANES), lambda bi, hb, i: (bi, 0, hb)),
        ],
        out_specs=pl.BlockSpec((None, tq, LANES), lambda bi, hb, i: (bi, i, hb)),
        out_shape=jax.ShapeDtypeStruct((b, s, d), BF16),
        scratch_shapes=[pltpu.SMEM((HEADS_PER_BLOCK,), F32)],
        compiler_params=pltpu.CompilerParams(
            dimension_semantics=("parallel", "parallel", "arbitrary"),
            vmem_limit_bytes=VMEM_LIMIT),
        name="sb_prompt",
    )(bias, q_bf, kt_bf, v_bf)


SAMPLE_PARTS = ("new_rows", "pages", "output")


def _sb_sample_kernel(pt_ref, bias_ref, q_ref, kn_ref, vn_ref, *rest, n_pages_step, t_new,
                      step=None, parts=SAMPLE_PARTS):
    k_refs = rest[:n_pages_step]
    v_refs = rest[n_pages_step:2 * n_pages_step]
    o_ref, carry_ref, acc_ref = rest[2 * n_pages_step:]
    seq, g, n_g = step or (pl.program_id(0), pl.program_id(1), pl.num_programs(1))
    q_rows = q_ref[...]
    bias = bias_ref[...]

    def lane_tiled(x, n):
        return x[:, :n] if n <= LANES else jnp.concatenate([x] * (n // LANES), axis=1)

    def tile(s, mask, weigh_values, carry, acc):
        n = s.shape[1]
        z = s + lane_tiled(bias, n)
        sp = _softplus2(z)
        if mask is not None:
            sp = jnp.where(mask, sp, 0.0)
        hi, lo = _split_bf16(sp)
        c = jnp.dot(jnp.concatenate([hi, lo], axis=1), _neg_suffix_sum_weights(n),
                    preferred_element_type=F32)
        c = c + lane_tiled(carry, n)
        a = jnp.exp2(z + c)
        if mask is not None:
            a = jnp.where(mask, a, 0.0)
        return (jnp.broadcast_to(c[:, 0:1], carry.shape), acc + weigh_values(a.astype(BF16)))

    if "new_rows" in parts:
        @pl.when(g == 0)
        def _():
            kn, vn = kn_ref[...], vn_ref[...]
            n = kn.shape[1]
            qry_i = lax.broadcasted_iota(jnp.int32, (LANES, n), 0) % t_new
            key_i = lax.broadcasted_iota(jnp.int32, (LANES, n), 1)
            mask = (key_i // t_new == seq) & (key_i % t_new < qry_i)
            carry_ref[...], acc_ref[...] = tile(
                jnp.dot(q_rows, kn, preferred_element_type=F32), mask,
                lambda a: jnp.dot(a, vn, preferred_element_type=F32),
                jnp.zeros(carry_ref.shape, F32), jnp.zeros(acc_ref.shape, F32))

    if "pages" in parts:
        carry, acc = carry_ref[...], acc_ref[...]
        for lo_page in reversed(range(0, n_pages_step, SAMPLE_TILE_PAGES)):
            sel = slice(lo_page, lo_page + SAMPLE_TILE_PAGES)
            kt = jnp.concatenate([r[...].astype(BF16) for r in k_refs[sel]], axis=1)
            vt = jnp.concatenate([r[...].astype(BF16) for r in v_refs[sel]], axis=1)
            carry, acc = tile(jnp.dot(q_rows, kt, preferred_element_type=F32), None,
                              lambda a: lax.dot_general(a, vt, (((1,), (1,)), ((), ())),
                                                        preferred_element_type=F32), carry, acc)
        carry_ref[...], acc_ref[...] = carry, acc

    if "output" in parts:
        @pl.when(g == n_g - 1)
        def _():
            acc = acc_ref[...]
            r = lax.broadcasted_iota(jnp.int32, acc.shape, 0) // t_new
            c = lax.broadcasted_iota(jnp.int32, acc.shape, 1) // HEAD_DIM
            own = jnp.where(r == c, acc, 0.0).reshape(N_HEADS, t_new, D_MODEL)
            o_ref[...] = jnp.sum(own, axis=0).astype(o_ref.dtype)


def _sb_sample(q_rows, kt_new, v_new, cache_k, cache_v, layer, page_table, bias, *, t_new,
               pages_per_step):
    n, _, d = q_rows.shape
    n_pages = page_table.shape[1]
    assert n_pages % pages_per_step == 0
    n_steps = n_pages // pages_per_step
    n_layers, n_pool = cache_k.shape[:2]
    ck = cache_k.transpose(0, 1, 3, 4, 2).reshape(n_layers, n_pool, d, PAGE_SIZE)
    cv = cache_v.transpose(0, 1, 3, 4, 2).reshape(n_layers, n_pool, d, PAGE_SIZE)
    bias_rows = jnp.broadcast_to(jnp.repeat(bias.astype(F32), t_new)[:, None], (LANES, LANES))

    def page_spec(i):
        return pl.BlockSpec(
            (None, None, d, PAGE_SIZE),
            lambda s, g, pt: (layer, pt[s, (n_steps - 1 - g) * pages_per_step + i], 0, 0))

    whole = lambda a: pl.BlockSpec(a.shape, lambda s, g, pt: (0,) * a.ndim)
    grid_spec = pltpu.PrefetchScalarGridSpec(
        num_scalar_prefetch=1,
        grid=(n, n_steps),
        in_specs=[whole(bias_rows), pl.BlockSpec((None, LANES, d), lambda s, g, pt: (s, 0, 0)),
                  whole(kt_new), whole(v_new)]
                 + [page_spec(i) for i in range(pages_per_step)] * 2,
        out_specs=pl.BlockSpec((None, t_new, d), lambda s, g, pt: (s, 0, 0)),
        scratch_shapes=[pltpu.VMEM((LANES, LANES), F32), pltpu.VMEM((LANES, d), F32)],
    )
    return pl.pallas_call(
        functools.partial(_sb_sample_kernel, n_pages_step=pages_per_step, t_new=t_new),
        grid_spec=grid_spec,
        out_shape=jax.ShapeDtypeStruct((n, t_new, d), BF16),
        compiler_params=pltpu.CompilerParams(
            dimension_semantics=("parallel", "arbitrary"), vmem_limit_bytes=VMEM_LIMIT),
        name="sb_sample",
    )(page_table, bias_rows, q_rows, kt_new, v_new,
      *([ck] * pages_per_step), *([cv] * pages_per_step))


def _sb_fused_kernel(pt_ref, pbias_ref, q_ref, kt_ref, v_ref, sbias_ref, qrows_ref, kn_ref, vn_ref,
                     *rest, tq, tk, n_pages_step, t_new, steps_per_seq):
    pages = rest[:2 * n_pages_step]
    o_ref, os_ref, knorm_ref, carry_ref, acc_ref = rest[2 * n_pages_step:]
    lin = ((pl.program_id(0) * pl.num_programs(1) + pl.program_id(1)) * pl.num_programs(2)
           + pl.program_id(2))
    sample = functools.partial(
        _sb_sample_kernel, pt_ref, sbias_ref, qrows_ref, kn_ref, vn_ref, *pages, os_ref, carry_ref,
        acc_ref, n_pages_step=n_pages_step, t_new=t_new,
        step=(lin // steps_per_seq, lin % steps_per_seq, steps_per_seq))
    sample(parts=("new_rows",))
    _sb_prompt_kernel(pbias_ref, q_ref, kt_ref, v_ref, o_ref, knorm_ref, tq=tq, tk=tk,
                      beside_band=functools.partial(sample, parts=("pages",)))
    sample(parts=("output",))


def _fused_steps_per_seq(q_bf, q_rows, page_table, tq):
    b, s, _ = q_bf.shape
    total = b * N_HEAD_BLOCKS * (s // tq)
    n_seq, n_pages = q_rows.shape[0], page_table.shape[1]
    if total % n_seq or n_pages % (total // n_seq):
        return None
    return total // n_seq


def _sb_fused(q_bf, kt_bf, v_bf, bias, q_rows, kt_new, v_new, cache_k, cache_v, layer, page_table,
              *, tq, t_new, steps_per_seq):
    b, s, d = q_bf.shape
    tk = kt_bf.shape[-1]
    n_q = s // tq
    n_seq, n_pages = q_rows.shape[0], page_table.shape[1]
    pages_per_step = n_pages // steps_per_seq
    assert tq % tk == 0 and s % tq == 0
    n_layers, n_pool = cache_k.shape[:2]
    ck = cache_k.transpose(0, 1, 3, 4, 2).reshape(n_layers, n_pool, d, PAGE_SIZE)
    cv = cache_v.transpose(0, 1, 3, 4, 2).reshape(n_layers, n_pool, d, PAGE_SIZE)
    bias_rows = jnp.broadcast_to(jnp.repeat(bias.astype(F32), t_new)[:, None], (LANES, LANES))

    def seq_step(bi, hb, i):
        lin = (bi * N_HEAD_BLOCKS + hb) * n_q + i
        return lin // steps_per_seq, lin % steps_per_seq

    def page_spec(k):
        def index(bi, hb, i, pt):
            seq, g = seq_step(bi, hb, i)
            return layer, pt[seq, (steps_per_seq - 1 - g) * pages_per_step + k], 0, 0
        return pl.BlockSpec((None, None, d, PAGE_SIZE), index)

    whole = lambda a: pl.BlockSpec(a.shape, lambda *_: (0,) * a.ndim)
    per_seq = lambda rows: pl.BlockSpec((None, rows, d),
                                        lambda bi, hb, i, pt: (seq_step(bi, hb, i)[0], 0, 0))
    grid_spec = pltpu.PrefetchScalarGridSpec(
        num_scalar_prefetch=1,
        grid=(b, N_HEAD_BLOCKS, n_q),
        in_specs=[pl.BlockSpec(memory_space=pltpu.SMEM),
                  pl.BlockSpec((None, tq, LANES), lambda bi, hb, i, pt: (bi, i, hb)),
                  pl.BlockSpec((None, s // tk, LANES, tk), lambda bi, hb, i, pt: (bi, 0, hb, 0)),
                  pl.BlockSpec((None, s, LANES), lambda bi, hb, i, pt: (bi, 0, hb)),
                  whole(bias_rows), per_seq(LANES), whole(kt_new), whole(v_new)]
                 + [page_spec(k) for k in range(pages_per_step)] * 2,
        out_specs=[pl.BlockSpec((None, tq, LANES), lambda bi, hb, i, pt: (bi, i, hb)),
                   per_seq(t_new)],
        scratch_shapes=[pltpu.SMEM((HEADS_PER_BLOCK,), F32),
                        pltpu.VMEM((LANES, LANES), F32), pltpu.VMEM((LANES, d), F32)],
    )
    return pl.pallas_call(
        functools.partial(_sb_fused_kernel, tq=tq, tk=tk, n_pages_step=pages_per_step,
                          t_new=t_new, steps_per_seq=steps_per_seq),
        grid_spec=grid_spec,
        out_shape=[jax.ShapeDtypeStruct((b, s, d), BF16),
                   jax.ShapeDtypeStruct((n_seq, t_new, d), BF16)],
        compiler_params=pltpu.CompilerParams(
            dimension_semantics=("arbitrary", "arbitrary", "arbitrary"),
            vmem_limit_bytes=VMEM_LIMIT),
        name="sb_fused",
    )(page_table, bias, q_bf, kt_bf, v_bf, bias_rows, q_rows, kt_new, v_new,
      *([ck] * pages_per_step), *([cv] * pages_per_step))


def _cm_kernel(x_ref, wuv_ref, g_ref, b_ref, ws_ref, bias_ref, gate_ref, *rest, chunk, with_v):
    rows = ws_ref.shape[-1]
    xb = x_ref[...].astype(BF16)
    z = jnp.dot(xb, wuv_ref[...], preferred_element_type=F32)
    z = 0.5 * z * (1.0 + lax.erf(z * (0.5 ** 0.5)))
    u = z[:, :CM_WIDTH]
    v = _layer_norm(z[:, CM_WIDTH:], g_ref[...], b_ref[...])
    if with_v:
        rest[0][...] = v
    r = lax.broadcasted_iota(jnp.int32, (rows, rows), 0)
    c = lax.broadcasted_iota(jnp.int32, (rows, rows), 1)
    keep = c <= r
    if chunk < rows:
        keep &= (r // chunk) == (c // chunk)
    w_sp = [jnp.where(keep, ws_ref[g], 0.0).astype(BF16) for g in range(CM_GROUPS)]
    vb = v.astype(BF16)
    bias = bias_ref[...]
    for ci in range(x_ref.shape[0] // rows):
        rs = slice(ci * rows, (ci + 1) * rows)
        for g in range(CM_GROUPS):
            cs = slice(g * CM_GROUP_DIM, (g + 1) * CM_GROUP_DIM)
            mixed = jnp.dot(w_sp[g], vb[rs, cs], preferred_element_type=F32) + bias[:, cs]
            gate_ref[rs, cs] = (u[rs, cs] * mixed).astype(gate_ref.dtype)


def _chunk_mlp(x, w_uv_bf, ln_g, ln_b, layer, w_sp, bias_full, *, tm, chunk, with_v):
    n, d = x.shape
    rows = w_sp.shape[-1]
    assert n % tm == 0 and tm % rows == 0 and rows % chunk == 0
    row_spec = pl.BlockSpec((tm, d), lambda i: (i, 0))
    out_shape = [jax.ShapeDtypeStruct((n, CM_WIDTH), BF16)]
    out_specs = [pl.BlockSpec((tm, CM_WIDTH), lambda i: (i, 0))]
    if with_v:
        out_shape.append(jax.ShapeDtypeStruct((n, CM_WIDTH), F32))
        out_specs.append(pl.BlockSpec((tm, CM_WIDTH), lambda i: (i, 0)))
    return pl.pallas_call(
        functools.partial(_cm_kernel, chunk=chunk, with_v=with_v),
        grid=(n // tm,),
        in_specs=[row_spec, _layer_spec(w_uv_bf, layer), _layer_spec(ln_g, layer),
                  _layer_spec(ln_b, layer), _const_spec(w_sp.shape), _const_spec(bias_full.shape)],
        out_specs=out_specs,
        out_shape=out_shape,
        compiler_params=pltpu.CompilerParams(
            dimension_semantics=("parallel",), vmem_limit_bytes=VMEM_LIMIT),
        name="chunk_mlp",
    )(x, w_uv_bf, ln_g, ln_b, w_sp, bias_full)


def _tail_kernel(x_ref, pre_ref, p_ref, wo_ref, g1_ref, b1_ref, win_ref, wout_ref,
                 g2_ref, b2_ref, wg_ref, bg_ref, wp_ref, o_ref):
    mix = jnp.dot(pre_ref[...], wo_ref[...], preferred_element_type=F32)
    h = _layer_norm(DEEPNORM_ALPHA * x_ref[...] + mix, g1_ref[...], b1_ref[...])
    gu = jnp.dot(h.astype(BF16), win_ref[...], preferred_element_type=F32)
    gate_half = gu[:, :D_FF]
    act = gate_half * jax.nn.sigmoid(gate_half) * gu[:, D_FF:]
    f = jnp.dot(act.astype(BF16), wout_ref[...], preferred_element_type=F32)
    h = _layer_norm(DEEPNORM_ALPHA * h + f, g2_ref[...], b2_ref[...])
    ple_gate = jax.nn.sigmoid(
        jnp.dot(h.astype(BF16), wg_ref[...], preferred_element_type=F32) + bg_ref[...])
    ple = jnp.dot(p_ref[...].astype(BF16), wp_ref[...], preferred_element_type=F32)
    o_ref[...] = h + ple_gate * ple


def _layer_tail(x, pre_bf, p, layer, wo, mixer_layer, params, *, tm):
    n, d = x.shape
    assert n % tm == 0
    row = lambda w: pl.BlockSpec((tm, w), lambda i: (i, 0))
    return pl.pallas_call(
        _tail_kernel,
        grid=(n // tm,),
        in_specs=[row(d), row(d), pl.BlockSpec((None, tm, PLE_DIM), lambda i: (layer, i, 0)),
                  _layer_spec(wo, mixer_layer)] + [_layer_spec(a, layer) for a in params],
        out_specs=row(d),
        out_shape=jax.ShapeDtypeStruct((n, d), F32),
        compiler_params=pltpu.CompilerParams(
            dimension_semantics=("parallel",), vmem_limit_bytes=VMEM_LIMIT),
        name="layer_tail",
    )(x, pre_bf, p, wo, *params)


def _tile_sizes(n_rows, preferred):
    t = min(preferred, n_rows)
    while n_rows % t:
        t -= 8
    return t


def kernel(x_prompt, x_sample, cache_k, cache_v, page_table, p_prompt, p_sample, w_qkv, w_o_sb,
           sb_bias, w_uv, ln_v_g, ln_v_b, w_s, b_s, w_o_cm, ln1_g, ln1_b, ln2_g, ln2_b,
           w_ffn_in, w_ffn_out, w_ple_gate, b_ple_gate, w_ple_proj):
    b, s, d = x_prompt.shape
    n_seq, t_new, _ = x_sample.shape
    n_p, n_s = b * s, n_seq * t_new
    depth = w_ffn_in.shape[0]
    assert depth == DEPTH and s % CHUNK == 0
    attn_tile = MXU_TILE
    attn_tq = _tile_sizes(s, ATTN_Q_ROWS)
    tm_p = _tile_sizes(s, ROW_TILE)
    pages_per_step = next(p for p in (16, 8, 4, 2, 1) if page_table.shape[1] % p == 0)

    vec = lambda a: a.reshape(a.shape[0], 1, a.shape[1])
    w_qkv_bf, w_uv_bf = w_qkv.astype(BF16), w_uv.astype(BF16)
    w_o_sb_bf, w_o_cm_bf = w_o_sb.astype(BF16), w_o_cm.astype(BF16)
    ln_v_g3, ln_v_b3 = vec(ln_v_g), vec(ln_v_b)
    tail_params = (vec(ln1_g), vec(ln1_b), w_ffn_in.astype(BF16), w_ffn_out.astype(BF16),
                   vec(ln2_g), vec(ln2_b), w_ple_gate.astype(BF16), vec(b_ple_gate),
                   w_ple_proj.astype(BF16))
    p_prompt2 = p_prompt.reshape(depth, n_p, PLE_DIM)
    p_sample2 = p_sample.reshape(depth, n_s, PLE_DIM)

    xp = x_prompt
    xs = x_sample.reshape(n_s, d)
    kt_all = vt_all = None
    ks_list, vs_list, cv_list = [], [], []
    for i in range(depth):
        j = i // 2
        if i % 2 == 0:
            q_bf, v_bf, kt_all, vt_all, kt_bf = _qkv_prompt(xp, w_qkv_bf, j, kt_all, vt_all,
                                                             tm=tm_p, tk=attn_tile)
            bias2 = sb_bias[j] * LOG2E
            q_rows, kn, vn, knt_bf, vn_bf = _qkv_sample(xs, w_qkv_bf, j, t_new=t_new)
            steps_per_seq = _fused_steps_per_seq(q_bf, q_rows, page_table, attn_tq)
            if steps_per_seq:
                pre_p, pre_s = _sb_fused(q_bf, kt_bf, v_bf, bias2, q_rows, knt_bf, vn_bf, cache_k,
                                         cache_v, j, page_table, tq=attn_tq, t_new=t_new,
                                         steps_per_seq=steps_per_seq)
            else:
                pre_p = _sb_prompt(q_bf, kt_bf, v_bf, bias2, tq=attn_tq)
                pre_s = _sb_sample(q_rows, knt_bf, vn_bf, cache_k, cache_v, j, page_table, bias2,
                                   t_new=t_new, pages_per_step=pages_per_step)
            pre_p, pre_s = pre_p.reshape(n_p, d), pre_s.reshape(n_s, d)
            ks_list.append(kn.reshape(n_seq, t_new, N_HEADS, HEAD_DIM))
            vs_list.append(vn.reshape(n_seq, t_new, N_HEADS, HEAD_DIM))
            w_o = w_o_sb_bf
        else:
            bias_p = jnp.repeat(b_s[j].T, CM_GROUP_DIM, axis=1)
            pre_p, = _chunk_mlp(xp.reshape(n_p, d), w_uv_bf, ln_v_g3, ln_v_b3, j, w_s[j], bias_p,
                                tm=tm_p, chunk=CHUNK, with_v=False)
            w_rep = jnp.tile(w_s[j][:, :t_new, :t_new], (1, n_seq, n_seq))
            bias_s = jnp.tile(jnp.repeat(b_s[j].T[:t_new], CM_GROUP_DIM, axis=1), (n_seq, 1))
            pre_s, v_s = _chunk_mlp(xs, w_uv_bf, ln_v_g3, ln_v_b3, j, w_rep, bias_s,
                                    tm=n_s, chunk=t_new, with_v=True)
            cv_list.append(v_s.reshape(n_seq, t_new, CM_WIDTH))
            w_o = w_o_cm_bf

        xp = _layer_tail(xp.reshape(n_p, d), pre_p, p_prompt2, i, w_o, j, tail_params,
                         tm=_tile_sizes(n_p, ROW_TILE)).reshape(b, s, d)
        xs = _layer_tail(xs, pre_s, p_sample2, i, w_o, j, tail_params,
                         tm=_tile_sizes(n_s, ROW_TILE))
    heads_last = lambda a: a.reshape(-1, b, N_HEADS, HEAD_DIM, s).transpose(0, 1, 4, 2, 3)
    return (xp, xs.reshape(n_seq, t_new, d), heads_last(kt_all), heads_last(vt_all),
            jnp.stack(ks_list), jnp.stack(vs_list), jnp.stack(cv_list))
```

```python
import functools

import jax
import jax.numpy as jnp
from jax import lax
from jax.experimental import pallas as pl
from jax.experimental.pallas import tpu as pltpu

D_MODEL = 1024
N_HEADS = 16
HEAD_DIM = 64
PAGE_SIZE = 128
CHUNK = 128
CM_WIDTH = D_MODEL
CM_GROUPS = 8
CM_GROUP_DIM = CM_WIDTH // CM_GROUPS
D_FF = 2816
PLE_DIM = 256
DEPTH = 4
DEEPNORM_ALPHA = (2 * DEPTH) ** 0.25
LN_EPS = 1e-5
LOG2E = 1.4426950408889634

LANES = 128
MXU_TILE = 256
ROW_TILE = 512
ATTN_Q_ROWS = 1024
NORM_SLACK = 1.01
HEADS_PER_BLOCK = LANES // HEAD_DIM
N_HEAD_BLOCKS = N_HEADS // HEADS_PER_BLOCK
VMEM_LIMIT = 56 * 1024 * 1024
SAMPLE_TILE_PAGES = 4
EXIT_LOG2 = 160.0
EXP2_CAP = 120.0

F32 = jnp.float32
BF16 = jnp.bfloat16


def _const_spec(shape):
    nd = len(shape)
    return pl.BlockSpec(shape, lambda *_: (0,) * nd, pipeline_mode=pl.Buffered(1))


def _layer_spec(stacked, layer):
    nd = stacked.ndim
    return pl.BlockSpec((None,) + stacked.shape[1:], lambda *_: (layer,) + (0,) * (nd - 1),
                        pipeline_mode=pl.Buffered(1))


def _layer_norm(x, g, b):
    mu = jnp.mean(x, axis=-1, keepdims=True)
    xc = x - mu
    var = jnp.mean(xc * xc, axis=-1, keepdims=True)
    return xc * lax.rsqrt(var + LN_EPS) * g + b


def _softplus2(z2):
    return jnp.maximum(jnp.log2(1.0 + jnp.exp2(jnp.minimum(z2, EXP2_CAP))), z2)


def _neg_suffix_sum_weights(n):
    r = lax.broadcasted_iota(jnp.int32, (n, n), 0)
    c = lax.broadcasted_iota(jnp.int32, (n, n), 1)
    return jnp.where(r >= c, -1.0, 0.0).astype(BF16)


Q_SCALE = HEAD_DIM ** -0.5 * LOG2E


def _qkv_prompt_kernel(x_ref, w_ref, *rest, tk, first):
    q_ref, vb_ref, kt_ref, vt_ref, ktb_ref = rest if first else rest[2:]
    xb = x_ref[...].astype(BF16)
    qkv = jnp.dot(xb, w_ref[...], preferred_element_type=F32)
    k = qkv[:, D_MODEL:2 * D_MODEL]
    v = qkv[:, 2 * D_MODEL:]
    q_ref[...] = (qkv[:, :D_MODEL] * Q_SCALE).astype(BF16)
    vb_ref[...] = v.astype(BF16)
    kt, vt = k.T, v.T
    if first:
        for l in range(kt_ref.shape[0]):
            kt_ref[l] = kt
            vt_ref[l] = vt
    else:
        kt_ref[...] = kt
        vt_ref[...] = vt
    for j in range(ktb_ref.shape[0]):
        ktb_ref[j] = kt[:, j * tk:(j + 1) * tk].astype(BF16)


def _qkv_prompt(x, w_bf, layer, kt_all, vt_all, *, tm, tk):
    b, s, d = x.shape
    n_layers = w_bf.shape[0]
    first = kt_all is None
    assert s % tm == 0 and tm % tk == 0 and first == (layer == 0)
    row_spec = pl.BlockSpec((None, tm, d), lambda bi, i: (bi, i, 0))
    if first:
        stack_spec = pl.BlockSpec((n_layers, None, d, tm), lambda bi, i: (0, bi, 0, i))
        stacks, stack_in_specs, aliases = (), [], {}
    else:
        stack_spec = pl.BlockSpec((None, None, d, tm), lambda bi, i: (layer, bi, 0, i))
        stacks, stack_in_specs = (kt_all, vt_all), [pl.BlockSpec(memory_space=pl.ANY)] * 2
        aliases = {2: 2, 3: 3}
    stack_shape = jax.ShapeDtypeStruct((n_layers, b, d, s), F32)
    return pl.pallas_call(
        functools.partial(_qkv_prompt_kernel, tk=tk, first=first),
        grid=(b, s // tm),
        in_specs=[row_spec, _layer_spec(w_bf, layer)] + stack_in_specs,
        out_specs=[row_spec, row_spec, stack_spec, stack_spec,
                   pl.BlockSpec((None, tm // tk, d, tk), lambda bi, i: (bi, i, 0, 0))],
        out_shape=[jax.ShapeDtypeStruct((b, s, d), BF16), jax.ShapeDtypeStruct((b, s, d), BF16),
                   stack_shape, stack_shape, jax.ShapeDtypeStruct((b, s // tk, d, tk), BF16)],
        input_output_aliases=aliases,
        compiler_params=pltpu.CompilerParams(
            dimension_semantics=("parallel", "parallel"), vmem_limit_bytes=VMEM_LIMIT),
        name="qkv_prompt",
    )(x, w_bf, *stacks)


def _qkv_sample_kernel(x_ref, w_ref, qrows_ref, k_ref, v_ref, ktb_ref, vb_ref, *, t_new):
    n_s = x_ref.shape[0]
    n_seq = n_s // t_new
    xb = x_ref[...].astype(BF16)
    qkv = jnp.dot(xb, w_ref[...], preferred_element_type=F32)
    k = qkv[:, D_MODEL:2 * D_MODEL]
    v = qkv[:, 2 * D_MODEL:]
    k_ref[...] = k
    v_ref[...] = v
    ktb_ref[...] = k.T.astype(BF16)
    vb_ref[...] = v.astype(BF16)
    q4 = (qkv[:, :D_MODEL] * Q_SCALE).reshape(n_seq, 1, t_new, D_MODEL)
    rows = jnp.broadcast_to(q4, (n_seq, N_HEADS, t_new, D_MODEL)).reshape(n_seq, LANES, D_MODEL)
    r = lax.broadcasted_iota(jnp.int32, (n_seq, LANES, D_MODEL), 1) // t_new
    c = lax.broadcasted_iota(jnp.int32, (n_seq, LANES, D_MODEL), 2) // HEAD_DIM
    qrows_ref[...] = jnp.where(r == c, rows, 0.0).astype(BF16)


def _qkv_sample(x, w_bf, layer, *, t_new):
    n_s, d = x.shape
    assert N_HEADS * t_new == LANES
    whole = lambda shape: pl.BlockSpec(shape, lambda i: (0,) * len(shape))
    out_shape = [jax.ShapeDtypeStruct((n_s // t_new, LANES, d), BF16),
                 jax.ShapeDtypeStruct((n_s, d), F32), jax.ShapeDtypeStruct((n_s, d), F32),
                 jax.ShapeDtypeStruct((d, n_s), BF16), jax.ShapeDtypeStruct((n_s, d), BF16)]
    return pl.pallas_call(
        functools.partial(_qkv_sample_kernel, t_new=t_new),
        grid=(1,),
        in_specs=[whole((n_s, d)), _layer_spec(w_bf, layer)],
        out_specs=[whole(o.shape) for o in out_shape],
        out_shape=out_shape,
        compiler_params=pltpu.CompilerParams(
            dimension_semantics=("arbitrary",), vmem_limit_bytes=VMEM_LIMIT),
        name="qkv_sample",
    )(x, w_bf)


def _sb_prompt_kernel(bias_ref, q_ref, kt_ref, v_ref, o_ref, knorm_ref, *, tq, tk,
                      beside_band=None):
    hb = pl.program_id(1)
    qi = pl.program_id(2)
    heads = range(HEADS_PER_BLOCK)

    @pl.when(qi == 0)
    def _():
        kt_all = kt_ref[...].astype(F32)
        k_sq = kt_all * kt_all
        for h in heads:
            per_key = jnp.sum(k_sq[:, h * HEAD_DIM:(h + 1) * HEAD_DIM, :], axis=1)
            knorm_ref[h] = jnp.max(per_key)

    n_band = tq // tk
    tiles_per_trip = n_band
    q2 = q_ref[...]
    qlane_i = lax.broadcasted_iota(jnp.int32, q2.shape, 1)
    qlane = qlane_i // HEAD_DIM
    vlane = lax.broadcasted_iota(jnp.int32, (tk, LANES), 1) // HEAD_DIM
    krow_i = lax.broadcasted_iota(jnp.int32, (LANES, tk), 0)
    biases = [bias_ref[hb * HEADS_PER_BLOCK + h] for h in heads]
    w_cum = _neg_suffix_sum_weights(tk)
    q_heads, k_own_rows, k_bias_rows = [], [], []
    for h in heads:
        spare = (HEADS_PER_BLOCK - 1 - h) * HEAD_DIM
        ones = ((qlane_i == spare) | (qlane_i == spare + 1)).astype(BF16)
        q_heads.append(jnp.where(qlane == h, q2, ones))
        b_full = jnp.zeros((LANES, tk), F32) + biases[h]
        b_hi = b_full.astype(BF16)
        b_lo = (b_full - b_hi.astype(F32)).astype(BF16)
        k_own_rows.append(jnp.where(krow_i // HEAD_DIM == h, 1.0, 0.0).astype(BF16))
        k_bias_rows.append(jnp.where(krow_i == spare, b_full,
                                     jnp.where(krow_i == spare + 1, b_full - b_hi.astype(F32), 0.0)
                                     ).astype(BF16))

    def tile(kb, carry, r0):
        carries, acc = carry
        masked = r0 is not None
        r0 = r0 or 0
        rows = tq - r0
        keep_head_rows = lambda old, new: jnp.concatenate([old[:r0], new], axis=0) if r0 else new
        kt2 = kt_ref[kb]
        v2 = v_ref[pl.ds(pl.multiple_of(kb * tk, tk), tk), :]
        if masked:
            causal = (lax.broadcasted_iota(jnp.int32, (rows, tk), 1)
                      < lax.broadcasted_iota(jnp.int32, (rows, tk), 0))
        new_carries, a_parts = [], []
        for h in heads:
            kt_h = kt2 * k_own_rows[h] + k_bias_rows[h]
            z = jnp.dot(q_heads[h][r0:], kt_h, preferred_element_type=F32)
            sp = _softplus2(z)
            if masked:
                sp = jnp.where(causal, sp, 0.0)
            c = jnp.dot(sp.astype(BF16), w_cum,
                        preferred_element_type=F32) + carries[h][r0:]
            a = jnp.exp2(z + c)
            if masked:
                a = jnp.where(causal, a, 0.0)
            a_parts.append(a.astype(BF16))
            new_carries.append(keep_head_rows(carries[h], c[:, 0:1]))
        v_st = jnp.concatenate([jnp.where(vlane == h, v2, jnp.zeros_like(v2)) for h in heads], axis=0)
        upd = acc[r0:] + jnp.dot(jnp.concatenate(a_parts, axis=1), v_st, preferred_element_type=F32)
        return tuple(new_carries), keep_head_rows(acc, upd)

    carry = (tuple(jnp.zeros((tq, 1), F32) for _ in heads), jnp.zeros((tq, LANES), F32))
    first_kb = qi * n_band
    for j in reversed(range(n_band)):
        carry = tile(first_kb + j, carry, j * tk)
    if beside_band is not None:
        beside_band()

    q_sq = q2.astype(F32) * q2.astype(F32)
    z_max = []
    for h in heads:
        q_norm2 = jnp.sum(jnp.where(qlane == h, q_sq, 0.0), axis=1, keepdims=True)
        z_max.append(jnp.sqrt(q_norm2 * knorm_ref[h]) * NORM_SLACK + jnp.abs(jnp.zeros((tq, 1), F32)
                                                                     + biases[h]))

    def log2_weight_bound(carries):
        return jnp.max(jnp.maximum(*(carries[h] + z_max[h] for h in heads)))

    def more_to_do(state):
        t, bound, _ = state
        return (t < first_kb // tiles_per_trip) & (bound > -EXIT_LOG2)

    def past_tiles(state):
        t, _, cr = state
        for j in range(tiles_per_trip):
            cr = tile(first_kb - 1 - (t * tiles_per_trip + j), cr, None)
        return t + 1, log2_weight_bound(cr[0]), cr

    _, _, carry = lax.while_loop(more_to_do, past_tiles,
                                 (jnp.int32(0), log2_weight_bound(carry[0]), carry))
    o_ref[...] = carry[1].astype(o_ref.dtype)


def _sb_prompt(q_bf, kt_bf, v_bf, bias, *, tq):
    b, s, d = q_bf.shape
    tk = kt_bf.shape[-1]
    assert tq % tk == 0 and s % tq == 0
    grid = (b, N_HEAD_BLOCKS, s // tq)
    return pl.pallas_call(
        functools.partial(_sb_prompt_kernel, tq=tq, tk=tk),
        grid=grid,
        in_specs=[
            pl.BlockSpec(memory_space=pltpu.SMEM),
            pl.BlockSpec((None, tq, LANES), lambda bi, hb, i: (bi, i, hb)),
            pl.BlockSpec((None, s // tk, LANES, tk), lambda bi, hb, i: (bi, 0, hb, 0)),
            pl.BlockSpec((None, s, LANES), lambda bi, hb, i: (bi, 0, hb)),
        ],
        out_specs=pl.BlockSpec((None, tq, LANES), lambda bi, hb, i: (bi, i, hb)),
        out_shape=jax.ShapeDtypeStruct((b, s, d), BF16),
        scratch_shapes=[pltpu.SMEM((HEADS_PER_BLOCK,), F32)],
        compiler_params=pltpu.CompilerParams(
            dimension_semantics=("parallel", "parallel", "arbitrary"),
            vmem_limit_bytes=VMEM_LIMIT),
        name="sb_prompt",
    )(bias, q_bf, kt_bf, v_bf)


SAMPLE_PARTS = ("new_rows", "pages", "output")


def _sb_sample_kernel(pt_ref, bias_ref, q_ref, kn_ref, vn_ref, *rest, n_pages_step, t_new,
                      step=None, parts=SAMPLE_PARTS):
    k_refs = rest[:n_pages_step]
    v_refs = rest[n_pages_step:2 * n_pages_step]
    o_ref, carry_ref, acc_ref = rest[2 * n_pages_step:]
    seq, g, n_g = step or (pl.program_id(0), pl.program_id(1), pl.num_programs(1))
    q_rows = q_ref[...]
    bias = bias_ref[...]

    def lane_tiled(x, n):
        return x[:, :n] if n <= LANES else jnp.concatenate([x] * (n // LANES), axis=1)

    def tile(s, mask, weigh_values, carry, acc):
        n = s.shape[1]
        z = s + lane_tiled(bias, n)
        sp = _softplus2(z)
        if mask is not None:
            sp = jnp.where(mask, sp, 0.0)
        c = jnp.dot(sp.astype(BF16), _neg_suffix_sum_weights(n), preferred_element_type=F32)
        c = c + lane_tiled(carry, n)
        a = jnp.exp2(z + c)
        if mask is not None:
            a = jnp.where(mask, a, 0.0)
        return (jnp.broadcast_to(c[:, 0:1], carry.shape), acc + weigh_values(a.astype(BF16)))

    if "new_rows" in parts:
        @pl.when(g == 0)
        def _():
            kn, vn = kn_ref[...], vn_ref[...]
            n = kn.shape[1]
            qry_i = lax.broadcasted_iota(jnp.int32, (LANES, n), 0) % t_new
            key_i = lax.broadcasted_iota(jnp.int32, (LANES, n), 1)
            mask = (key_i // t_new == seq) & (key_i % t_new < qry_i)
            carry_ref[...], acc_ref[...] = tile(
                jnp.dot(q_rows, kn, preferred_element_type=F32), mask,
                lambda a: jnp.dot(a, vn, preferred_element_type=F32),
                jnp.zeros(carry_ref.shape, F32), jnp.zeros(acc_ref.shape, F32))

    if "pages" in parts:
        carry, acc = carry_ref[...], acc_ref[...]
        for lo_page in reversed(range(0, n_pages_step, SAMPLE_TILE_PAGES)):
            sel = slice(lo_page, lo_page + SAMPLE_TILE_PAGES)
            kt = jnp.concatenate([r[...].astype(BF16) for r in k_refs[sel]], axis=1)
            vt = jnp.concatenate([r[...].astype(BF16) for r in v_refs[sel]], axis=1)
            carry, acc = tile(jnp.dot(q_rows, kt, preferred_element_type=F32), None,
                              lambda a: lax.dot_general(a, vt, (((1,), (1,)), ((), ())),
                                                        preferred_element_type=F32), carry, acc)
        carry_ref[...], acc_ref[...] = carry, acc

    if "output" in parts:
        @pl.when(g == n_g - 1)
        def _():
            acc = acc_ref[...]
            r = lax.broadcasted_iota(jnp.int32, acc.shape, 0) // t_new
            c = lax.broadcasted_iota(jnp.int32, acc.shape, 1) // HEAD_DIM
            own = jnp.where(r == c, acc, 0.0).reshape(N_HEADS, t_new, D_MODEL)
            o_ref[...] = jnp.sum(own, axis=0).astype(o_ref.dtype)


def _sb_sample(q_rows, kt_new, v_new, cache_k, cache_v, layer, page_table, bias, *, t_new,
               pages_per_step):
    n, _, d = q_rows.shape
    n_pages = page_table.shape[1]
    assert n_pages % pages_per_step == 0
    n_steps = n_pages // pages_per_step
    n_layers, n_pool = cache_k.shape[:2]
    ck = cache_k.transpose(0, 1, 3, 4, 2).reshape(n_layers, n_pool, d, PAGE_SIZE)
    cv = cache_v.transpose(0, 1, 3, 4, 2).reshape(n_layers, n_pool, d, PAGE_SIZE)
    bias_rows = jnp.broadcast_to(jnp.repeat(bias.astype(F32), t_new)[:, None], (LANES, LANES))

    def page_spec(i):
        return pl.BlockSpec(
            (None, None, d, PAGE_SIZE),
            lambda s, g, pt: (layer, pt[s, (n_steps - 1 - g) * pages_per_step + i], 0, 0))

    whole = lambda a: pl.BlockSpec(a.shape, lambda s, g, pt: (0,) * a.ndim)
    grid_spec = pltpu.PrefetchScalarGridSpec(
        num_scalar_prefetch=1,
        grid=(n, n_steps),
        in_specs=[whole(bias_rows), pl.BlockSpec((None, LANES, d), lambda s, g, pt: (s, 0, 0)),
                  whole(kt_new), whole(v_new)]
                 + [page_spec(i) for i in range(pages_per_step)] * 2,
        out_specs=pl.BlockSpec((None, t_new, d), lambda s, g, pt: (s, 0, 0)),
        scratch_shapes=[pltpu.VMEM((LANES, LANES), F32), pltpu.VMEM((LANES, d), F32)],
    )
    return pl.pallas_call(
        functools.partial(_sb_sample_kernel, n_pages_step=pages_per_step, t_new=t_new),
        grid_spec=grid_spec,
        out_shape=jax.ShapeDtypeStruct((n, t_new, d), BF16),
        compiler_params=pltpu.CompilerParams(
            dimension_semantics=("parallel", "arbitrary"), vmem_limit_bytes=VMEM_LIMIT),
        name="sb_sample",
    )(page_table, bias_rows, q_rows, kt_new, v_new,
      *([ck] * pages_per_step), *([cv] * pages_per_step))


def _sb_fused_kernel(pt_ref, pbias_ref, q_ref, kt_ref, v_ref, sbias_ref, qrows_ref, kn_ref, vn_ref,
                     *rest, tq, tk, n_pages_step, t_new, steps_per_seq):
    pages = rest[:2 * n_pages_step]
    o_ref, os_ref, knorm_ref, carry_ref, acc_ref = rest[2 * n_pages_step:]
    lin = ((pl.program_id(0) * pl.num_programs(1) + pl.program_id(1)) * pl.num_programs(2)
           + pl.program_id(2))
    sample = functools.partial(
        _sb_sample_kernel, pt_ref, sbias_ref, qrows_ref, kn_ref, vn_ref, *pages, os_ref, carry_ref,
        acc_ref, n_pages_step=n_pages_step, t_new=t_new,
        step=(lin // steps_per_seq, lin % steps_per_seq, steps_per_seq))
    sample(parts=("new_rows",))
    _sb_prompt_kernel(pbias_ref, q_ref, kt_ref, v_ref, o_ref, knorm_ref, tq=tq, tk=tk,
                      beside_band=functools.partial(sample, parts=("pages",)))
    sample(parts=("output",))


def _fused_steps_per_seq(q_bf, q_rows, page_table, tq):
    b, s, _ = q_bf.shape
    total = b * N_HEAD_BLOCKS * (s // tq)
    n_seq, n_pages = q_rows.shape[0], page_table.shape[1]
    if total % n_seq or n_pages % (total // n_seq):
        return None
    return total // n_seq


def _sb_fused(q_bf, kt_bf, v_bf, bias, q_rows, kt_new, v_new, cache_k, cache_v, layer, page_table,
              *, tq, t_new, steps_per_seq):
    b, s, d = q_bf.shape
    tk = kt_bf.shape[-1]
    n_q = s // tq
    n_seq, n_pages = q_rows.shape[0], page_table.shape[1]
    pages_per_step = n_pages // steps_per_seq
    assert tq % tk == 0 and s % tq == 0
    n_layers, n_pool = cache_k.shape[:2]
    ck = cache_k.transpose(0, 1, 3, 4, 2).reshape(n_layers, n_pool, d, PAGE_SIZE)
    cv = cache_v.transpose(0, 1, 3, 4, 2).reshape(n_layers, n_pool, d, PAGE_SIZE)
    bias_rows = jnp.broadcast_to(jnp.repeat(bias.astype(F32), t_new)[:, None], (LANES, LANES))

    def seq_step(bi, hb, i):
        lin = (bi * N_HEAD_BLOCKS + hb) * n_q + i
        return lin // steps_per_seq, lin % steps_per_seq

    def page_spec(k):
        def index(bi, hb, i, pt):
            seq, g = seq_step(bi, hb, i)
            return layer, pt[seq, (steps_per_seq - 1 - g) * pages_per_step + k], 0, 0
        return pl.BlockSpec((None, None, d, PAGE_SIZE), index)

    whole = lambda a: pl.BlockSpec(a.shape, lambda *_: (0,) * a.ndim)
    per_seq = lambda rows: pl.BlockSpec((None, rows, d),
                                        lambda bi, hb, i, pt: (seq_step(bi, hb, i)[0], 0, 0))
    grid_spec = pltpu.PrefetchScalarGridSpec(
        num_scalar_prefetch=1,
        grid=(b, N_HEAD_BLOCKS, n_q),
        in_specs=[pl.BlockSpec(memory_space=pltpu.SMEM),
                  pl.BlockSpec((None, tq, LANES), lambda bi, hb, i, pt: (bi, i, hb)),
                  pl.BlockSpec((None, s // tk, LANES, tk), lambda bi, hb, i, pt: (bi, 0, hb, 0)),
                  pl.BlockSpec((None, s, LANES), lambda bi, hb, i, pt: (bi, 0, hb)),
                  whole(bias_rows), per_seq(LANES), whole(kt_new), whole(v_new)]
                 + [page_spec(k) for k in range(pages_per_step)] * 2,
        out_specs=[pl.BlockSpec((None, tq, LANES), lambda bi, hb, i, pt: (bi, i, hb)),
                   per_seq(t_new)],
        scratch_shapes=[pltpu.SMEM((HEADS_PER_BLOCK,), F32),
                        pltpu.VMEM((LANES, LANES), F32), pltpu.VMEM((LANES, d), F32)],
    )
    return pl.pallas_call(
        functools.partial(_sb_fused_kernel, tq=tq, tk=tk, n_pages_step=pages_per_step,
                          t_new=t_new, steps_per_seq=steps_per_seq),
        grid_spec=grid_spec,
        out_shape=[jax.ShapeDtypeStruct((b, s, d), BF16),
                   jax.ShapeDtypeStruct((n_seq, t_new, d), BF16)],
        compiler_params=pltpu.CompilerParams(
            dimension_semantics=("arbitrary", "arbitrary", "arbitrary"),
            vmem_limit_bytes=VMEM_LIMIT),
        name="sb_fused",
    )(page_table, bias, q_bf, kt_bf, v_bf, bias_rows, q_rows, kt_new, v_new,
      *([ck] * pages_per_step), *([cv] * pages_per_step))


def _cm_kernel(x_ref, wuv_ref, g_ref, b_ref, ws_ref, bias_ref, gate_ref, *rest, chunk, with_v):
    rows = ws_ref.shape[-1]
    xb = x_ref[...].astype(BF16)
    z = jnp.dot(xb, wuv_ref[...], preferred_element_type=F32)
    z = 0.5 * z * (1.0 + lax.erf(z * (0.5 ** 0.5)))
    u = z[:, :CM_WIDTH]
    v = _layer_norm(z[:, CM_WIDTH:], g_ref[...], b_ref[...])
    if with_v:
        rest[0][...] = v
    r = lax.broadcasted_iota(jnp.int32, (rows, rows), 0)
    c = lax.broadcasted_iota(jnp.int32, (rows, rows), 1)
    keep = c <= r
    if chunk < rows:
        keep &= (r // chunk) == (c // chunk)
    w_sp = [jnp.where(keep, ws_ref[g], 0.0).astype(BF16) for g in range(CM_GROUPS)]
    vb = v.astype(BF16)
    bias = bias_ref[...]
    for ci in range(x_ref.shape[0] // rows):
        rs = slice(ci * rows, (ci + 1) * rows)
        for g in range(CM_GROUPS):
            cs = slice(g * CM_GROUP_DIM, (g + 1) * CM_GROUP_DIM)
            mixed = jnp.dot(w_sp[g], vb[rs, cs], preferred_element_type=F32) + bias[:, cs]
            gate_ref[rs, cs] = (u[rs, cs] * mixed).astype(gate_ref.dtype)


def _chunk_mlp(x, w_uv_bf, ln_g, ln_b, layer, w_sp, bias_full, *, tm, chunk, with_v):
    n, d = x.shape
    rows = w_sp.shape[-1]
    assert n % tm == 0 and tm % rows == 0 and rows % chunk == 0
    row_spec = pl.BlockSpec((tm, d), lambda i: (i, 0))
    out_shape = [jax.ShapeDtypeStruct((n, CM_WIDTH), BF16)]
    out_specs = [pl.BlockSpec((tm, CM_WIDTH), lambda i: (i, 0))]
    if with_v:
        out_shape.append(jax.ShapeDtypeStruct((n, CM_WIDTH), F32))
        out_specs.append(pl.BlockSpec((tm, CM_WIDTH), lambda i: (i, 0)))
    return pl.pallas_call(
        functools.partial(_cm_kernel, chunk=chunk, with_v=with_v),
        grid=(n // tm,),
        in_specs=[row_spec, _layer_spec(w_uv_bf, layer), _layer_spec(ln_g, layer),
                  _layer_spec(ln_b, layer), _const_spec(w_sp.shape), _const_spec(bias_full.shape)],
        out_specs=out_specs,
        out_shape=out_shape,
        compiler_params=pltpu.CompilerParams(
            dimension_semantics=("parallel",), vmem_limit_bytes=VMEM_LIMIT),
        name="chunk_mlp",
    )(x, w_uv_bf, ln_g, ln_b, w_sp, bias_full)


def _tail_kernel(x_ref, pre_ref, p_ref, wo_ref, g1_ref, b1_ref, win_ref, wout_ref,
                 g2_ref, b2_ref, wg_ref, bg_ref, wp_ref, o_ref):
    mix = jnp.dot(pre_ref[...], wo_ref[...], preferred_element_type=F32)
    h = _layer_norm(DEEPNORM_ALPHA * x_ref[...] + mix, g1_ref[...], b1_ref[...])
    gu = jnp.dot(h.astype(BF16), win_ref[...], preferred_element_type=F32)
    gate_half = gu[:, :D_FF]
    act = gate_half * jax.nn.sigmoid(gate_half) * gu[:, D_FF:]
    f = jnp.dot(act.astype(BF16), wout_ref[...], preferred_element_type=F32)
    h = _layer_norm(DEEPNORM_ALPHA * h + f, g2_ref[...], b2_ref[...])
    ple_gate = jax.nn.sigmoid(
        jnp.dot(h.astype(BF16), wg_ref[...], preferred_element_type=F32) + bg_ref[...])
    ple = jnp.dot(p_ref[...].astype(BF16), wp_ref[...], preferred_element_type=F32)
    o_ref[...] = h + ple_gate * ple


def _layer_tail(x, pre_bf, p, layer, wo, mixer_layer, params, *, tm):
    n, d = x.shape
    assert n % tm == 0
    row = lambda w: pl.BlockSpec((tm, w), lambda i: (i, 0))
    return pl.pallas_call(
        _tail_kernel,
        grid=(n // tm,),
        in_specs=[row(d), row(d), pl.BlockSpec((None, tm, PLE_DIM), lambda i: (layer, i, 0)),
                  _layer_spec(wo, mixer_layer)] + [_layer_spec(a, layer) for a in params],
        out_specs=row(d),
        out_shape=jax.ShapeDtypeStruct((n, d), F32),
        compiler_params=pltpu.CompilerParams(
            dimension_semantics=("parallel",), vmem_limit_bytes=VMEM_LIMIT),
        name="layer_tail",
    )(x, pre_bf, p, wo, *params)


def _tile_sizes(n_rows, preferred):
    t = min(preferred, n_rows)
    while n_rows % t:
        t -= 8
    return t


def kernel(x_prompt, x_sample, cache_k, cache_v, page_table, p_prompt, p_sample, w_qkv, w_o_sb,
           sb_bias, w_uv, ln_v_g, ln_v_b, w_s, b_s, w_o_cm, ln1_g, ln1_b, ln2_g, ln2_b,
           w_ffn_in, w_ffn_out, w_ple_gate, b_ple_gate, w_ple_proj):
    b, s, d = x_prompt.shape
    n_seq, t_new, _ = x_sample.shape
    n_p, n_s = b * s, n_seq * t_new
    depth = w_ffn_in.shape[0]
    assert depth == DEPTH and s % CHUNK == 0
    attn_tile = MXU_TILE
    attn_tq = _tile_sizes(s, ATTN_Q_ROWS)
    tm_p = _tile_sizes(s, ROW_TILE)
    pages_per_step = next(p for p in (16, 8, 4, 2, 1) if page_table.shape[1] % p == 0)

    vec = lambda a: a.reshape(a.shape[0], 1, a.shape[1])
    w_qkv_bf, w_uv_bf = w_qkv.astype(BF16), w_uv.astype(BF16)
    w_o_sb_bf, w_o_cm_bf = w_o_sb.astype(BF16), w_o_cm.astype(BF16)
    ln_v_g3, ln_v_b3 = vec(ln_v_g), vec(ln_v_b)
    tail_params = (vec(ln1_g), vec(ln1_b), w_ffn_in.astype(BF16), w_ffn_out.astype(BF16),
                   vec(ln2_g), vec(ln2_b), w_ple_gate.astype(BF16), vec(b_ple_gate),
                   w_ple_proj.astype(BF16))
    p_prompt2 = p_prompt.reshape(depth, n_p, PLE_DIM)
    p_sample2 = p_sample.reshape(depth, n_s, PLE_DIM)

    xp = x_prompt
    xs = x_sample.reshape(n_s, d)
    kt_all = vt_all = None
    ks_list, vs_list, cv_list = [], [], []
    for i in range(depth):
        j = i // 2
        if i % 2 == 0:
            q_bf, v_bf, kt_all, vt_all, kt_bf = _qkv_prompt(xp, w_qkv_bf, j, kt_all, vt_all,
                                                             tm=tm_p, tk=attn_tile)
            bias2 = sb_bias[j] * LOG2E
            q_rows, kn, vn, knt_bf, vn_bf = _qkv_sample(xs, w_qkv_bf, j, t_new=t_new)
            steps_per_seq = _fused_steps_per_seq(q_bf, q_rows, page_table, attn_tq)
            if steps_per_seq:
                pre_p, pre_s = _sb_fused(q_bf, kt_bf, v_bf, bias2, q_rows, knt_bf, vn_bf, cache_k,
                                         cache_v, j, page_table, tq=attn_tq, t_new=t_new,
                                         steps_per_seq=steps_per_seq)
            else:
                pre_p = _sb_prompt(q_bf, kt_bf, v_bf, bias2, tq=attn_tq)
                pre_s = _sb_sample(q_rows, knt_bf, vn_bf, cache_k, cache_v, j, page_table, bias2,
                                   t_new=t_new, pages_per_step=pages_per_step)
            pre_p, pre_s = pre_p.reshape(n_p, d), pre_s.reshape(n_s, d)
            ks_list.append(kn.reshape(n_seq, t_new, N_HEADS, HEAD_DIM))
            vs_list.append(vn.reshape(n_seq, t_new, N_HEADS, HEAD_DIM))
            w_o = w_o_sb_bf
        else:
            bias_p = jnp.repeat(b_s[j].T, CM_GROUP_DIM, axis=1)
            pre_p, = _chunk_mlp(xp.reshape(n_p, d), w_uv_bf, ln_v_g3, ln_v_b3, j, w_s[j], bias_p,
                                tm=tm_p, chunk=CHUNK, with_v=False)
            w_rep = jnp.tile(w_s[j][:, :t_new, :t_new], (1, n_seq, n_seq))
            bias_s = jnp.tile(jnp.repeat(b_s[j].T[:t_new], CM_GROUP_DIM, axis=1), (n_seq, 1))
            pre_s, v_s = _chunk_mlp(xs, w_uv_bf, ln_v_g3, ln_v_b3, j, w_rep, bias_s,
                                    tm=n_s, chunk=t_new, with_v=True)
            cv_list.append(v_s.reshape(n_seq, t_new, CM_WIDTH))
            w_o = w_o_cm_bf

        xp = _layer_tail(xp.reshape(n_p, d), pre_p, p_prompt2, i, w_o, j, tail_params,
                         tm=_tile_sizes(n_p, ROW_TILE)).reshape(b, s, d)
        xs = _layer_tail(xs, pre_s, p_sample2, i, w_o, j, tail_params,
                         tm=_tile_sizes(n_s, ROW_TILE))
    heads_last = lambda a: a.reshape(-1, b, N_HEADS, HEAD_DIM, s).transpose(0, 1, 4, 2, 3)
    return (xp, xs.reshape(n_seq, t_new, d), heads_last(kt_all), heads_last(vt_all),
            jnp.stack(ks_list), jnp.stack(vs_list), jnp.stack(cv_list))
```

```python
import functools

import jax
import jax.numpy as jnp
from jax import lax
from jax.experimental import pallas as pl
from jax.experimental.pallas import tpu as pltpu

D_MODEL = 1024
N_HEADS = 16
HEAD_DIM = 64
PAGE_SIZE = 128
CHUNK = 128
CM_WIDTH = D_MODEL
CM_GROUPS = 8
CM_GROUP_DIM = CM_WIDTH // CM_GROUPS
D_FF = 2816
PLE_DIM = 256
DEPTH = 4
DEEPNORM_ALPHA = (2 * DEPTH) ** 0.25
LN_EPS = 1e-5
LOG2E = 1.4426950408889634

LANES = 128
MXU_TILE = 256
ROW_TILE = 512
FFN_SLICE = 6 * MXU_TILE
ATTN_Q_ROWS = 1024
NORM_SLACK = 1.01
HEADS_PER_BLOCK = LANES // HEAD_DIM
N_HEAD_BLOCKS = N_HEADS // HEADS_PER_BLOCK
VMEM_LIMIT = 56 * 1024 * 1024
SAMPLE_TILE_PAGES = 4
EXIT_LOG2 = 160.0
EXP2_CAP = 120.0

F32 = jnp.float32
BF16 = jnp.bfloat16


def _const_spec(shape):
    nd = len(shape)
    return pl.BlockSpec(shape, lambda *_: (0,) * nd, pipeline_mode=pl.Buffered(1))


def _layer_spec(stacked, layer):
    nd = stacked.ndim
    return pl.BlockSpec((None,) + stacked.shape[1:], lambda *_: (layer,) + (0,) * (nd - 1),
                        pipeline_mode=pl.Buffered(1))


def _layer_norm(x, g, b):
    mu = jnp.mean(x, axis=-1, keepdims=True)
    xc = x - mu
    var = jnp.mean(xc * xc, axis=-1, keepdims=True)
    return xc * lax.rsqrt(var + LN_EPS) * g + b


def _softplus2(z2):
    return jnp.maximum(jnp.log2(1.0 + jnp.exp2(jnp.minimum(z2, EXP2_CAP))), z2)


def _neg_suffix_sum_weights(n):
    r = lax.broadcasted_iota(jnp.int32, (n, n), 0)
    c = lax.broadcasted_iota(jnp.int32, (n, n), 1)
    return jnp.where(r >= c, -1.0, 0.0).astype(BF16)


Q_SCALE = HEAD_DIM ** -0.5 * LOG2E


def _qkv_prompt_kernel(x_ref, w_ref, *rest, tk, first):
    q_ref, vb_ref, kt_ref, vt_ref, ktb_ref = rest if first else rest[2:]
    xb = x_ref[...].astype(BF16)
    qkv = jnp.dot(xb, w_ref[...], preferred_element_type=F32)
    k = qkv[:, D_MODEL:2 * D_MODEL]
    v = qkv[:, 2 * D_MODEL:]
    q_ref[...] = (qkv[:, :D_MODEL] * Q_SCALE).astype(BF16)
    vb_ref[...] = v.astype(BF16)
    kt, vt = k.T, v.T
    if first:
        for l in range(kt_ref.shape[0]):
            kt_ref[l] = kt
            vt_ref[l] = vt
    else:
        kt_ref[...] = kt
        vt_ref[...] = vt
    for j in range(ktb_ref.shape[0]):
        ktb_ref[j] = kt[:, j * tk:(j + 1) * tk].astype(BF16)


def _qkv_prompt(x, w_bf, layer, kt_all, vt_all, *, tm, tk):
    b, s, d = x.shape
    n_layers = w_bf.shape[0]
    first = kt_all is None
    assert s % tm == 0 and tm % tk == 0 and first == (layer == 0)
    row_spec = pl.BlockSpec((None, tm, d), lambda bi, i: (bi, i, 0))
    if first:
        stack_spec = pl.BlockSpec((n_layers, None, d, tm), lambda bi, i: (0, bi, 0, i))
        stacks, stack_in_specs, aliases = (), [], {}
    else:
        stack_spec = pl.BlockSpec((None, None, d, tm), lambda bi, i: (layer, bi, 0, i))
        stacks, stack_in_specs = (kt_all, vt_all), [pl.BlockSpec(memory_space=pl.ANY)] * 2
        aliases = {2: 2, 3: 3}
    stack_shape = jax.ShapeDtypeStruct((n_layers, b, d, s), F32)
    return pl.pallas_call(
        functools.partial(_qkv_prompt_kernel, tk=tk, first=first),
        grid=(b, s // tm),
        in_specs=[row_spec, _layer_spec(w_bf, layer)] + stack_in_specs,
        out_specs=[row_spec, row_spec, stack_spec, stack_spec,
                   pl.BlockSpec((None, tm // tk, d, tk), lambda bi, i: (bi, i, 0, 0))],
        out_shape=[jax.ShapeDtypeStruct((b, s, d), BF16), jax.ShapeDtypeStruct((b, s, d), BF16),
                   stack_shape, stack_shape, jax.ShapeDtypeStruct((b, s // tk, d, tk), BF16)],
        input_output_aliases=aliases,
        compiler_params=pltpu.CompilerParams(
            dimension_semantics=("parallel", "parallel"), vmem_limit_bytes=VMEM_LIMIT),
        name="qkv_prompt",
    )(x, w_bf, *stacks)


def _qkv_sample_kernel(x_ref, w_ref, qrows_ref, k_ref, v_ref, ktb_ref, vb_ref, *, t_new):
    n_s = x_ref.shape[0]
    n_seq = n_s // t_new
    xb = x_ref[...].astype(BF16)
    qkv = jnp.dot(xb, w_ref[...], preferred_element_type=F32)
    k = qkv[:, D_MODEL:2 * D_MODEL]
    v = qkv[:, 2 * D_MODEL:]
    k_ref[...] = k
    v_ref[...] = v
    ktb_ref[...] = k.T.astype(BF16)
    vb_ref[...] = v.astype(BF16)
    q4 = (qkv[:, :D_MODEL] * Q_SCALE).reshape(n_seq, 1, t_new, D_MODEL)
    rows = jnp.broadcast_to(q4, (n_seq, N_HEADS, t_new, D_MODEL)).reshape(n_seq, LANES, D_MODEL)
    r = lax.broadcasted_iota(jnp.int32, (n_seq, LANES, D_MODEL), 1) // t_new
    c = lax.broadcasted_iota(jnp.int32, (n_seq, LANES, D_MODEL), 2) // HEAD_DIM
    qrows_ref[...] = jnp.where(r == c, rows, 0.0).astype(BF16)


def _qkv_sample(x, w_bf, layer, *, t_new):
    n_s, d = x.shape
    assert N_HEADS * t_new == LANES
    whole = lambda shape: pl.BlockSpec(shape, lambda i: (0,) * len(shape))
    out_shape = [jax.ShapeDtypeStruct((n_s // t_new, LANES, d), BF16),
                 jax.ShapeDtypeStruct((n_s, d), F32), jax.ShapeDtypeStruct((n_s, d), F32),
                 jax.ShapeDtypeStruct((d, n_s), BF16), jax.ShapeDtypeStruct((n_s, d), BF16)]
    return pl.pallas_call(
        functools.partial(_qkv_sample_kernel, t_new=t_new),
        grid=(1,),
        in_specs=[whole((n_s, d)), _layer_spec(w_bf, layer)],
        out_specs=[whole(o.shape) for o in out_shape],
        out_shape=out_shape,
        compiler_params=pltpu.CompilerParams(
            dimension_semantics=("arbitrary",), vmem_limit_bytes=VMEM_LIMIT),
        name="qkv_sample",
    )(x, w_bf)


def _sb_prompt_kernel(bias_ref, q_ref, kt_ref, v_ref, o_ref, knorm_ref, *, tq, tk,
                      beside_band=None):
    hb = pl.program_id(1)
    qi = pl.program_id(2)
    heads = range(HEADS_PER_BLOCK)

    @pl.when(qi == 0)
    def _():
        kt_all = kt_ref[...].astype(F32)
        k_sq = kt_all * kt_all
        for h in heads:
            per_key = jnp.sum(k_sq[:, h * HEAD_DIM:(h + 1) * HEAD_DIM, :], axis=1)
            knorm_ref[h] = jnp.max(per_key)

    n_band = tq // tk
    tiles_per_trip = n_band
    q2 = q_ref[...]
    qlane_i = lax.broadcasted_iota(jnp.int32, q2.shape, 1)
    qlane = qlane_i // HEAD_DIM
    vlane = lax.broadcasted_iota(jnp.int32, (tk, LANES), 1) // HEAD_DIM
    krow_i = lax.broadcasted_iota(jnp.int32, (LANES, tk), 0)
    biases = [bias_ref[hb * HEADS_PER_BLOCK + h] for h in heads]
    w_cum = _neg_suffix_sum_weights(tk)
    q_heads, k_own_rows, k_bias_rows = [], [], []
    for h in heads:
        spare = (HEADS_PER_BLOCK - 1 - h) * HEAD_DIM
        ones = ((qlane_i == spare) | (qlane_i == spare + 1)).astype(BF16)
        q_heads.append(jnp.where(qlane == h, q2, ones))
        b_full = jnp.zeros((LANES, tk), F32) + biases[h]
        b_hi = b_full.astype(BF16)
        b_lo = (b_full - b_hi.astype(F32)).astype(BF16)
        k_own_rows.append(jnp.where(krow_i // HEAD_DIM == h, 1.0, 0.0).astype(BF16))
        k_bias_rows.append(jnp.where(krow_i == spare, b_full,
                                     jnp.where(krow_i == spare + 1, b_full - b_hi.astype(F32), 0.0)
                                     ).astype(BF16))

    def tile(kb, carry, r0):
        carries, acc = carry
        masked = r0 is not None
        r0 = r0 or 0
        rows = tq - r0
        keep_head_rows = lambda old, new: jnp.concatenate([old[:r0], new], axis=0) if r0 else new
        kt2 = kt_ref[kb]
        v2 = v_ref[pl.ds(pl.multiple_of(kb * tk, tk), tk), :]
        if masked:
            causal = (lax.broadcasted_iota(jnp.int32, (rows, tk), 1)
                      < lax.broadcasted_iota(jnp.int32, (rows, tk), 0))
        new_carries, a_parts = [], []
        for h in heads:
            kt_h = kt2 * k_own_rows[h] + k_bias_rows[h]
            z = jnp.dot(q_heads[h][r0:], kt_h, preferred_element_type=F32)
            sp = _softplus2(z)
            if masked:
                sp = jnp.where(causal, sp, 0.0)
            c = jnp.dot(sp.astype(BF16), w_cum,
                        preferred_element_type=F32) + carries[h][r0:]
            a = jnp.exp2(z + c)
            if masked:
                a = jnp.where(causal, a, 0.0)
            a_parts.append(a.astype(BF16))
            new_carries.append(keep_head_rows(carries[h], c[:, 0:1]))
        v_st = jnp.concatenate([jnp.where(vlane == h, v2, jnp.zeros_like(v2)) for h in heads], axis=0)
        upd = acc[r0:] + jnp.dot(jnp.concatenate(a_parts, axis=1), v_st, preferred_element_type=F32)
        return tuple(new_carries), keep_head_rows(acc, upd)

    carry = (tuple(jnp.zeros((tq, 1), F32) for _ in heads), jnp.zeros((tq, LANES), F32))
    first_kb = qi * n_band
    for j in reversed(range(n_band)):
        carry = tile(first_kb + j, carry, j * tk)
    if beside_band is not None:
        beside_band()

    q_sq = q2.astype(F32) * q2.astype(F32)
    z_max = []
    for h in heads:
        q_norm2 = jnp.sum(jnp.where(qlane == h, q_sq, 0.0), axis=1, keepdims=True)
        z_max.append(jnp.sqrt(q_norm2 * knorm_ref[h]) * NORM_SLACK + jnp.abs(jnp.zeros((tq, 1), F32)
                                                                     + biases[h]))

    def log2_weight_bound(carries):
        return jnp.max(jnp.maximum(*(carries[h] + z_max[h] for h in heads)))

    def more_to_do(state):
        t, bound, _ = state
        return (t < first_kb // tiles_per_trip) & (bound > -EXIT_LOG2)

    def past_tiles(state):
        t, _, cr = state
        for j in range(tiles_per_trip):
            cr = tile(first_kb - 1 - (t * tiles_per_trip + j), cr, None)
        return t + 1, log2_weight_bound(cr[0]), cr

    _, _, carry = lax.while_loop(more_to_do, past_tiles,
                                 (jnp.int32(0), log2_weight_bound(carry[0]), carry))
    o_ref[...] = carry[1].astype(o_ref.dtype)


def _sb_prompt(q_bf, kt_bf, v_bf, bias, *, tq):
    b, s, d = q_bf.shape
    tk = kt_bf.shape[-1]
    assert tq % tk == 0 and s % tq == 0
    grid = (b, N_HEAD_BLOCKS, s // tq)
    return pl.pallas_call(
        functools.partial(_sb_prompt_kernel, tq=tq, tk=tk),
        grid=grid,
        in_specs=[
            pl.BlockSpec(memory_space=pltpu.SMEM),
            pl.BlockSpec((None, tq, LANES), lambda bi, hb, i: (bi, i, hb)),
            pl.BlockSpec((None, s // tk, LANES, tk), lambda bi, hb, i: (bi, 0, hb, 0)),
            pl.BlockSpec((None, s, LANES), lambda bi, hb, i: (bi, 0, hb)),
        ],
        out_specs=pl.BlockSpec((None, tq, LANES), lambda bi, hb, i: (bi, i, hb)),
        out_shape=jax.ShapeDtypeStruct((b, s, d), BF16),
        scratch_shapes=[pltpu.SMEM((HEADS_PER_BLOCK,), F32)],
        compiler_params=pltpu.CompilerParams(
            dimension_semantics=("parallel", "parallel", "arbitrary"),
            vmem_limit_bytes=VMEM_LIMIT),
        name="sb_prompt",
    )(bias, q_bf, kt_bf, v_bf)


SAMPLE_PARTS = ("new_rows", "pages", "output")


def _sb_sample_kernel(pt_ref, bias_ref, q_ref, kn_ref, vn_ref, *rest, n_pages_step, t_new,
                      step=None, parts=SAMPLE_PARTS):
    k_refs = rest[:n_pages_step]
    v_refs = rest[n_pages_step:2 * n_pages_step]
    o_ref, carry_ref, acc_ref = rest[2 * n_pages_step:]
    seq, g, n_g = step or (pl.program_id(0), pl.program_id(1), pl.num_programs(1))
    q_rows = q_ref[...]
    bias = bias_ref[...]

    def lane_tiled(x, n):
        return x[:, :n] if n <= LANES else jnp.concatenate([x] * (n // LANES), axis=1)

    def tile(s, mask, weigh_values, carry, acc):
        n = s.shape[1]
        z = s + lane_tiled(bias, n)
        sp = _softplus2(z)
        if mask is not None:
            sp = jnp.where(mask, sp, 0.0)
        c = jnp.dot(sp.astype(BF16), _neg_suffix_sum_weights(n), preferred_element_type=F32)
        c = c + lane_tiled(carry, n)
        a = jnp.exp2(z + c)
        if mask is not None:
            a = jnp.where(mask, a, 0.0)
        return (jnp.broadcast_to(c[:, 0:1], carry.shape), acc + weigh_values(a.astype(BF16)))

    if "new_rows" in parts:
        @pl.when(g == 0)
        def _():
            kn, vn = kn_ref[...], vn_ref[...]
            n = kn.shape[1]
            qry_i = lax.broadcasted_iota(jnp.int32, (LANES, n), 0) % t_new
            key_i = lax.broadcasted_iota(jnp.int32, (LANES, n), 1)
            mask = (key_i // t_new == seq) & (key_i % t_new < qry_i)
            carry_ref[...], acc_ref[...] = tile(
                jnp.dot(q_rows, kn, preferred_element_type=F32), mask,
                lambda a: jnp.dot(a, vn, preferred_element_type=F32),
                jnp.zeros(carry_ref.shape, F32), jnp.zeros(acc_ref.shape, F32))

    if "pages" in parts:
        carry, acc = carry_ref[...], acc_ref[...]
        for lo_page in reversed(range(0, n_pages_step, SAMPLE_TILE_PAGES)):
            sel = slice(lo_page, lo_page + SAMPLE_TILE_PAGES)
            kt = jnp.concatenate([r[...].astype(BF16) for r in k_refs[sel]], axis=1)
            vt = jnp.concatenate([r[...].astype(BF16) for r in v_refs[sel]], axis=1)
            carry, acc = tile(jnp.dot(q_rows, kt, preferred_element_type=F32), None,
                              lambda a: lax.dot_general(a, vt, (((1,), (1,)), ((), ())),
                                                        preferred_element_type=F32), carry, acc)
        carry_ref[...], acc_ref[...] = carry, acc

    if "output" in parts:
        @pl.when(g == n_g - 1)
        def _():
            acc = acc_ref[...]
            r = lax.broadcasted_iota(jnp.int32, acc.shape, 0) // t_new
            c = lax.broadcasted_iota(jnp.int32, acc.shape, 1) // HEAD_DIM
            own = jnp.where(r == c, acc, 0.0).reshape(N_HEADS, t_new, D_MODEL)
            o_ref[...] = jnp.sum(own, axis=0).astype(o_ref.dtype)


def _sb_sample(q_rows, kt_new, v_new, cache_k, cache_v, layer, page_table, bias, *, t_new,
               pages_per_step):
    n, _, d = q_rows.shape
    n_pages = page_table.shape[1]
    assert n_pages % pages_per_step == 0
    n_steps = n_pages // pages_per_step
    n_layers, n_pool = cache_k.shape[:2]
    ck = cache_k.transpose(0, 1, 3, 4, 2).reshape(n_layers, n_pool, d, PAGE_SIZE)
    cv = cache_v.transpose(0, 1, 3, 4, 2).reshape(n_layers, n_pool, d, PAGE_SIZE)
    bias_rows = jnp.broadcast_to(jnp.repeat(bias.astype(F32), t_new)[:, None], (LANES, LANES))

    def page_spec(i):
        return pl.BlockSpec(
            (None, None, d, PAGE_SIZE),
            lambda s, g, pt: (layer, pt[s, (n_steps - 1 - g) * pages_per_step + i], 0, 0))

    whole = lambda a: pl.BlockSpec(a.shape, lambda s, g, pt: (0,) * a.ndim)
    grid_spec = pltpu.PrefetchScalarGridSpec(
        num_scalar_prefetch=1,
        grid=(n, n_steps),
        in_specs=[whole(bias_rows), pl.BlockSpec((None, LANES, d), lambda s, g, pt: (s, 0, 0)),
                  whole(kt_new), whole(v_new)]
                 + [page_spec(i) for i in range(pages_per_step)] * 2,
        out_specs=pl.BlockSpec((None, t_new, d), lambda s, g, pt: (s, 0, 0)),
        scratch_shapes=[pltpu.VMEM((LANES, LANES), F32), pltpu.VMEM((LANES, d), F32)],
    )
    return pl.pallas_call(
        functools.partial(_sb_sample_kernel, n_pages_step=pages_per_step, t_new=t_new),
        grid_spec=grid_spec,
        out_shape=jax.ShapeDtypeStruct((n, t_new, d), BF16),
        compiler_params=pltpu.CompilerParams(
            dimension_semantics=("parallel", "arbitrary"), vmem_limit_bytes=VMEM_LIMIT),
        name="sb_sample",
    )(page_table, bias_rows, q_rows, kt_new, v_new,
      *([ck] * pages_per_step), *([cv] * pages_per_step))


def _sb_fused_kernel(pt_ref, pbias_ref, q_ref, kt_ref, v_ref, sbias_ref, qrows_ref, kn_ref, vn_ref,
                     *rest, tq, tk, n_pages_step, t_new, steps_per_seq):
    pages = rest[:2 * n_pages_step]
    o_ref, os_ref, knorm_ref, carry_ref, acc_ref = rest[2 * n_pages_step:]
    lin = ((pl.program_id(0) * pl.num_programs(1) + pl.program_id(1)) * pl.num_programs(2)
           + pl.program_id(2))
    sample = functools.partial(
        _sb_sample_kernel, pt_ref, sbias_ref, qrows_ref, kn_ref, vn_ref, *pages, os_ref, carry_ref,
        acc_ref, n_pages_step=n_pages_step, t_new=t_new,
        step=(lin // steps_per_seq, lin % steps_per_seq, steps_per_seq))
    sample(parts=("new_rows",))
    _sb_prompt_kernel(pbias_ref, q_ref, kt_ref, v_ref, o_ref, knorm_ref, tq=tq, tk=tk,
                      beside_band=functools.partial(sample, parts=("pages",)))
    sample(parts=("output",))


def _fused_steps_per_seq(q_bf, q_rows, page_table, tq):
    b, s, _ = q_bf.shape
    total = b * N_HEAD_BLOCKS * (s // tq)
    n_seq, n_pages = q_rows.shape[0], page_table.shape[1]
    if total % n_seq or n_pages % (total // n_seq):
        return None
    return total // n_seq


def _sb_fused(q_bf, kt_bf, v_bf, bias, q_rows, kt_new, v_new, cache_k, cache_v, layer, page_table,
              *, tq, t_new, steps_per_seq):
    b, s, d = q_bf.shape
    tk = kt_bf.shape[-1]
    n_q = s // tq
    n_seq, n_pages = q_rows.shape[0], page_table.shape[1]
    pages_per_step = n_pages // steps_per_seq
    assert tq % tk == 0 and s % tq == 0
    n_layers, n_pool = cache_k.shape[:2]
    ck = cache_k.transpose(0, 1, 3, 4, 2).reshape(n_layers, n_pool, d, PAGE_SIZE)
    cv = cache_v.transpose(0, 1, 3, 4, 2).reshape(n_layers, n_pool, d, PAGE_SIZE)
    bias_rows = jnp.broadcast_to(jnp.repeat(bias.astype(F32), t_new)[:, None], (LANES, LANES))

    def seq_step(bi, hb, i):
        lin = (bi * N_HEAD_BLOCKS + hb) * n_q + i
        return lin // steps_per_seq, lin % steps_per_seq

    def page_spec(k):
        def index(bi, hb, i, pt):
            seq, g = seq_step(bi, hb, i)
            return layer, pt[seq, (steps_per_seq - 1 - g) * pages_per_step + k], 0, 0
        return pl.BlockSpec((None, None, d, PAGE_SIZE), index)

    whole = lambda a: pl.BlockSpec(a.shape, lambda *_: (0,) * a.ndim)
    per_seq = lambda rows: pl.BlockSpec((None, rows, d),
                                        lambda bi, hb, i, pt: (seq_step(bi, hb, i)[0], 0, 0))
    grid_spec = pltpu.PrefetchScalarGridSpec(
        num_scalar_prefetch=1,
        grid=(b, N_HEAD_BLOCKS, n_q),
        in_specs=[pl.BlockSpec(memory_space=pltpu.SMEM),
                  pl.BlockSpec((None, tq, LANES), lambda bi, hb, i, pt: (bi, i, hb)),
                  pl.BlockSpec((None, s // tk, LANES, tk), lambda bi, hb, i, pt: (bi, 0, hb, 0)),
                  pl.BlockSpec((None, s, LANES), lambda bi, hb, i, pt: (bi, 0, hb)),
                  whole(bias_rows), per_seq(LANES), whole(kt_new), whole(v_new)]
                 + [page_spec(k) for k in range(pages_per_step)] * 2,
        out_specs=[pl.BlockSpec((None, tq, LANES), lambda bi, hb, i, pt: (bi, i, hb)),
                   per_seq(t_new)],
        scratch_shapes=[pltpu.SMEM((HEADS_PER_BLOCK,), F32),
                        pltpu.VMEM((LANES, LANES), F32), pltpu.VMEM((LANES, d), F32)],
    )
    return pl.pallas_call(
        functools.partial(_sb_fused_kernel, tq=tq, tk=tk, n_pages_step=pages_per_step,
                          t_new=t_new, steps_per_seq=steps_per_seq),
        grid_spec=grid_spec,
        out_shape=[jax.ShapeDtypeStruct((b, s, d), BF16),
                   jax.ShapeDtypeStruct((n_seq, t_new, d), BF16)],
        compiler_params=pltpu.CompilerParams(
            dimension_semantics=("arbitrary", "arbitrary", "arbitrary"),
            vmem_limit_bytes=VMEM_LIMIT),
        name="sb_fused",
    )(page_table, bias, q_bf, kt_bf, v_bf, bias_rows, q_rows, kt_new, v_new,
      *([ck] * pages_per_step), *([cv] * pages_per_step))


def _cm_kernel(x_ref, wuv_ref, g_ref, b_ref, ws_ref, bias_ref, gate_ref, *rest, chunk, with_v):
    rows = ws_ref.shape[-1]
    xb = x_ref[...].astype(BF16)
    z = jnp.dot(xb, wuv_ref[...], preferred_element_type=F32)
    z = 0.5 * z * (1.0 + lax.erf(z * (0.5 ** 0.5)))
    u = z[:, :CM_WIDTH]
    v = _layer_norm(z[:, CM_WIDTH:], g_ref[...], b_ref[...])
    if with_v:
        rest[0][...] = v
    r = lax.broadcasted_iota(jnp.int32, (rows, rows), 0)
    c = lax.broadcasted_iota(jnp.int32, (rows, rows), 1)
    keep = c <= r
    if chunk < rows:
        keep &= (r // chunk) == (c // chunk)
    w_sp = [jnp.where(keep, ws_ref[g], 0.0).astype(BF16) for g in range(CM_GROUPS)]
    vb = v.astype(BF16)
    bias = bias_ref[...]
    for ci in range(x_ref.shape[0] // rows):
        rs = slice(ci * rows, (ci + 1) * rows)
        for g in range(CM_GROUPS):
            cs = slice(g * CM_GROUP_DIM, (g + 1) * CM_GROUP_DIM)
            mixed = jnp.dot(w_sp[g], vb[rs, cs], preferred_element_type=F32) + bias[:, cs]
            gate_ref[rs, cs] = (u[rs, cs] * mixed).astype(gate_ref.dtype)


def _chunk_mlp(x, w_uv_bf, ln_g, ln_b, layer, w_sp, bias_full, *, tm, chunk, with_v):
    n, d = x.shape
    rows = w_sp.shape[-1]
    assert n % tm == 0 and tm % rows == 0 and rows % chunk == 0
    row_spec = pl.BlockSpec((tm, d), lambda i: (i, 0))
    out_shape = [jax.ShapeDtypeStruct((n, CM_WIDTH), BF16)]
    out_specs = [pl.BlockSpec((tm, CM_WIDTH), lambda i: (i, 0))]
    if with_v:
        out_shape.append(jax.ShapeDtypeStruct((n, CM_WIDTH), F32))
        out_specs.append(pl.BlockSpec((tm, CM_WIDTH), lambda i: (i, 0)))
    return pl.pallas_call(
        functools.partial(_cm_kernel, chunk=chunk, with_v=with_v),
        grid=(n // tm,),
        in_specs=[row_spec, _layer_spec(w_uv_bf, layer), _layer_spec(ln_g, layer),
                  _layer_spec(ln_b, layer), _const_spec(w_sp.shape), _const_spec(bias_full.shape)],
        out_specs=out_specs,
        out_shape=out_shape,
        compiler_params=pltpu.CompilerParams(
            dimension_semantics=("parallel",), vmem_limit_bytes=VMEM_LIMIT),
        name="chunk_mlp",
    )(x, w_uv_bf, ln_g, ln_b, w_sp, bias_full)


def _tail_kernel(x_ref, pre_ref, p_ref, wo_ref, g1_ref, b1_ref, win_ref, wout_ref,
                 g2_ref, b2_ref, wg_ref, bg_ref, wp_ref, o_ref):
    mix = jnp.dot(pre_ref[...], wo_ref[...], preferred_element_type=F32)
    h = _layer_norm(DEEPNORM_ALPHA * x_ref[...] + mix, g1_ref[...], b1_ref[...])
    hb = h.astype(BF16)
    f = None
    for c0 in range(0, D_FF, FFN_SLICE):
        c1 = min(c0 + FFN_SLICE, D_FF)
        gate_half = jnp.dot(hb, win_ref[:, c0:c1], preferred_element_type=F32)
        up_half = jnp.dot(hb, win_ref[:, D_FF + c0:D_FF + c1], preferred_element_type=F32)
        act = gate_half * jax.nn.sigmoid(gate_half) * up_half
        part = jnp.dot(act.astype(BF16), wout_ref[c0:c1, :], preferred_element_type=F32)
        f = part if f is None else f + part
    h = _layer_norm(DEEPNORM_ALPHA * h + f, g2_ref[...], b2_ref[...])
    ple_gate = jax.nn.sigmoid(
        jnp.dot(h.astype(BF16), wg_ref[...], preferred_element_type=F32) + bg_ref[...])
    ple = jnp.dot(p_ref[...].astype(BF16), wp_ref[...], preferred_element_type=F32)
    o_ref[...] = h + ple_gate * ple


def _layer_tail(x, pre_bf, p, layer, wo, mixer_layer, params, *, tm):
    n, d = x.shape
    assert n % tm == 0
    row = lambda w: pl.BlockSpec((tm, w), lambda i: (i, 0))
    return pl.pallas_call(
        _tail_kernel,
        grid=(n // tm,),
        in_specs=[row(d), row(d), pl.BlockSpec((None, tm, PLE_DIM), lambda i: (layer, i, 0)),
                  _layer_spec(wo, mixer_layer)] + [_layer_spec(a, layer) for a in params],
        out_specs=row(d),
        out_shape=jax.ShapeDtypeStruct((n, d), F32),
        compiler_params=pltpu.CompilerParams(
            dimension_semantics=("parallel",), vmem_limit_bytes=VMEM_LIMIT),
        name="layer_tail",
    )(x, pre_bf, p, wo, *params)


def _tile_sizes(n_rows, preferred):
    t = min(preferred, n_rows)
    while n_rows % t:
        t -= 8
    return t


def kernel(x_prompt, x_sample, cache_k, cache_v, page_table, p_prompt, p_sample, w_qkv, w_o_sb,
           sb_bias, w_uv, ln_v_g, ln_v_b, w_s, b_s, w_o_cm, ln1_g, ln1_b, ln2_g, ln2_b,
           w_ffn_in, w_ffn_out, w_ple_gate, b_ple_gate, w_ple_proj):
    b, s, d = x_prompt.shape
    n_seq, t_new, _ = x_sample.shape
    n_p, n_s = b * s, n_seq * t_new
    depth = w_ffn_in.shape[0]
    assert depth == DEPTH and s % CHUNK == 0
    attn_tile = MXU_TILE
    attn_tq = _tile_sizes(s, ATTN_Q_ROWS)
    tm_p = _tile_sizes(s, ROW_TILE)
    pages_per_step = next(p for p in (16, 8, 4, 2, 1) if page_table.shape[1] % p == 0)

    vec = lambda a: a.reshape(a.shape[0], 1, a.shape[1])
    w_qkv_bf, w_uv_bf = w_qkv.astype(BF16), w_uv.astype(BF16)
    w_o_sb_bf, w_o_cm_bf = w_o_sb.astype(BF16), w_o_cm.astype(BF16)
    ln_v_g3, ln_v_b3 = vec(ln_v_g), vec(ln_v_b)
    tail_params = (vec(ln1_g), vec(ln1_b), w_ffn_in.astype(BF16), w_ffn_out.astype(BF16),
                   vec(ln2_g), vec(ln2_b), w_ple_gate.astype(BF16), vec(b_ple_gate),
                   w_ple_proj.astype(BF16))
    p_prompt2 = p_prompt.reshape(depth, n_p, PLE_DIM)
    p_sample2 = p_sample.reshape(depth, n_s, PLE_DIM)

    xp = x_prompt
    xs = x_sample.reshape(n_s, d)
    kt_all = vt_all = None
    ks_list, vs_list, cv_list = [], [], []
    for i in range(depth):
        j = i // 2
        if i % 2 == 0:
            q_bf, v_bf, kt_all, vt_all, kt_bf = _qkv_prompt(xp, w_qkv_bf, j, kt_all, vt_all,
                                                             tm=tm_p, tk=attn_tile)
            bias2 = sb_bias[j] * LOG2E
            q_rows, kn, vn, knt_bf, vn_bf = _qkv_sample(xs, w_qkv_bf, j, t_new=t_new)
            steps_per_seq = _fused_steps_per_seq(q_bf, q_rows, page_table, attn_tq)
            if steps_per_seq:
                pre_p, pre_s = _sb_fused(q_bf, kt_bf, v_bf, bias2, q_rows, knt_bf, vn_bf, cache_k,
                                         cache_v, j, page_table, tq=attn_tq, t_new=t_new,
                                         steps_per_seq=steps_per_seq)
            else:
                pre_p = _sb_prompt(q_bf, kt_bf, v_bf, bias2, tq=attn_tq)
                pre_s = _sb_sample(q_rows, knt_bf, vn_bf, cache_k, cache_v, j, page_table, bias2,
                                   t_new=t_new, pages_per_step=pages_per_step)
            pre_p, pre_s = pre_p.reshape(n_p, d), pre_s.reshape(n_s, d)
            ks_list.append(kn.reshape(n_seq, t_new, N_HEADS, HEAD_DIM))
            vs_list.append(vn.reshape(n_seq, t_new, N_HEADS, HEAD_DIM))
            w_o = w_o_sb_bf
        else:
            bias_p = jnp.repeat(b_s[j].T, CM_GROUP_DIM, axis=1)
            pre_p, = _chunk_mlp(xp.reshape(n_p, d), w_uv_bf, ln_v_g3, ln_v_b3, j, w_s[j], bias_p,
                                tm=tm_p, chunk=CHUNK, with_v=False)
            w_rep = jnp.tile(w_s[j][:, :t_new, :t_new], (1, n_seq, n_seq))
            bias_s = jnp.tile(jnp.repeat(b_s[j].T[:t_new], CM_GROUP_DIM, axis=1), (n_seq, 1))
            pre_s, v_s = _chunk_mlp(xs, w_uv_bf, ln_v_g3, ln_v_b3, j, w_rep, bias_s,
                                    tm=n_s, chunk=t_new, with_v=True)
            cv_list.append(v_s.reshape(n_seq, t_new, CM_WIDTH))
            w_o = w_o_cm_bf

        xp = _layer_tail(xp.reshape(n_p, d), pre_p, p_prompt2, i, w_o, j, tail_params,
                         tm=_tile_sizes(n_p, ROW_TILE)).reshape(b, s, d)
        xs = _layer_tail(xs, pre_s, p_sample2, i, w_o, j, tail_params,
                         tm=_tile_sizes(n_s, ROW_TILE))
    heads_last = lambda a: a.reshape(-1, b, N_HEADS, HEAD_DIM, s).transpose(0, 1, 4, 2, 3)
    return (xp, xs.reshape(n_seq, t_new, d), heads_last(kt_all), heads_last(vt_all),
            jnp.stack(ks_list), jnp.stack(vs_list), jnp.stack(cv_list))
```
